```python
import jax, jax.numpy as jnp
from jax import lax
import numpy as np

D_MODEL = 1024
BATCH = 8
SEQ = 2048
DEPTH = 2

HEAD_DIM = 64
ATTN_HEADS = D_MODEL // 128
ATTN_WIDTH = ATTN_HEADS * HEAD_DIM
MOBA_BLOCK = 256
MOBA_TOPK = 3
MOBA_Q_CHUNK = 32
POOL_WINDOWS = (2, 4, 8, 16)
POOL_GROUPS = len(POOL_WINDOWS)
POOL_WIDTH = D_MODEL // 2
POOL_GROUP_WIDTH = POOL_WIDTH // POOL_GROUPS
N_BRANCH = 2
IN_WIDTH = 3 * ATTN_WIDTH + POOL_WIDTH + N_BRANCH * D_MODEL
D_FF = ((8 * D_MODEL // 3 + 255) // 256) * 256
CONV_WIDTH = 3
RMS_EPS = 1e-6
NEG_INF = -1e30

kernel_name = "hybrid_moba_pool_convffn"


def rms_norm(x, g):
    xf = x.astype(jnp.float32)
    y = xf * lax.rsqrt(jnp.mean(xf * xf, axis=-1, keepdims=True) + RMS_EPS)
    return (y * g.astype(jnp.float32)).astype(x.dtype)


def moba_attention(q, k, v):
    B, S, H, Dh = q.shape
    nb = -(-S // MOBA_BLOCK)
    s_pad = nb * MOBA_BLOCK
    k_sel = min(MOBA_TOPK, nb)
    L = MOBA_BLOCK
    Q = MOBA_Q_CHUNK
    scale = Dh ** -0.5
    q = q.transpose(0, 2, 1, 3)
    pad = ((0, 0), (0, 0), (0, s_pad - S), (0, 0))
    k = jnp.pad(k.transpose(0, 2, 1, 3), pad)
    v = jnp.pad(v.transpose(0, 2, 1, 3), pad)
    k_blocks = k.reshape(B, H, nb, L, Dh)
    v_blocks = v.reshape(B, H, nb, L, Dh)
    counts = jnp.clip(S - jnp.arange(nb) * L, 1, L).astype(jnp.float32)
    k_mean = k_blocks.astype(jnp.float32).sum(axis=3) / counts[None, None, :, None]

    n_chunks = S // Q
    q_chunks = q.reshape(B, H, n_chunks, Q, Dh).transpose(2, 0, 1, 3, 4)
    b_idx = jnp.arange(B)[:, None, None, None]
    h_idx = jnp.arange(H)[None, :, None, None]

    def chunk_attn(args):
        c, qc = args
        q_pos = c * Q + jnp.arange(Q)
        own = (c * Q) // L
        gate = jnp.einsum('bhqd,bhnd->bhqn', qc.astype(jnp.float32), k_mean)
        gate = jnp.where(jnp.arange(nb) < own, gate, NEG_INF)
        _, sel = lax.top_k(gate, k_sel)
        sel_valid = jnp.arange(k_sel) < own
        k_g = k_blocks[b_idx, h_idx, sel]
        v_g = v_blocks[b_idx, h_idx, sel]
        k_own = lax.dynamic_slice_in_dim(k, own * L, L, axis=2)
        v_own = lax.dynamic_slice_in_dim(v, own * L, L, axis=2)
        s_sel = jnp.einsum('bhqd,bhqnld->bhqnl', qc, k_g).astype(jnp.float32) * scale
        s_sel = jnp.where(sel_valid[:, None], s_sel, NEG_INF)
        key_pos = own * L + jnp.arange(L)
        s_own = jnp.einsum('bhqd,bhld->bhql', qc, k_own).astype(jnp.float32) * scale
        s_own = jnp.where(key_pos[None, :] <= q_pos[:, None], s_own, NEG_INF)
        logits = jnp.concatenate([s_sel.reshape(B, H, Q, k_sel * L), s_own], axis=-1)
        p = jax.nn.softmax(logits, axis=-1).astype(v.dtype)
        p_sel = p[..., :k_sel * L].reshape(B, H, Q, k_sel, L)
        p_own = p[..., k_sel * L:]
        return (jnp.einsum('bhqnl,bhqnld->bhqd', p_sel, v_g)
                + jnp.einsum('bhql,bhld->bhqd', p_own, v_own))

    out = lax.map(chunk_attn, (jnp.arange(n_chunks), q_chunks))
    return out.transpose(1, 0, 3, 2, 4).reshape(B, S, H * Dh)


def multiscale_pool(u, w_pool, pool_scale):
    B, S, _ = u.shape
    uf = u.astype(jnp.float32).reshape(B, S, POOL_GROUPS, POOL_GROUP_WIDTH)
    cs = jnp.pad(jnp.cumsum(uf, axis=1), ((0, 0), (1, 0), (0, 0), (0, 0)))
    t = jnp.arange(S)
    win = jnp.array(POOL_WINDOWS, dtype=jnp.int32)
    lo = jnp.maximum(t[:, None] + 1 - win[None, :], 0)
    g_idx = jnp.arange(POOL_GROUPS)[None, :]
    window_sum = cs[:, 1:] - cs[:, lo, g_idx]
    count = (t[:, None] + 1 - lo).astype(jnp.float32)
    mixed = (window_sum / count[None, :, :, None] - uf).astype(u.dtype)
    y = jnp.einsum('bsgc,gcd->bsgd', mixed, w_pool).reshape(B, S, POOL_WIDTH)
    return y * pool_scale


def conv_gated_mlp(h, w_up, conv_w, conv_b, w_down):
    S = h.shape[1]
    a = h @ w_up
    ap = jnp.pad(a, ((0, 0), (CONV_WIDTH - 1, 0), (0, 0)))
    c = conv_b + sum(conv_w[j] * ap[:, j:j + S] for j in range(CONV_WIDTH))
    gate, val = jnp.split(c, 2, axis=-1)
    return (jax.nn.silu(gate) * val) @ w_down


def setup_inputs(seed: int = 0) -> dict:
    key = jax.random.key(seed)
    ks = jax.random.split(key, 16)
    f32 = jnp.float32
    n = lambda k, shape, fan_in: jax.random.normal(k, shape, f32) * (fan_in ** -0.5)
    return {
        "x": jax.random.normal(ks[0], (BATCH, SEQ, D_MODEL), f32),
        "norm_mix_g": 1.0 + 0.02 * jax.random.normal(ks[1], (DEPTH, D_MODEL), f32),
        "w_in": n(ks[2], (DEPTH, D_MODEL, IN_WIDTH), D_MODEL),
        "w_pool": n(ks[3], (DEPTH, POOL_GROUPS, POOL_GROUP_WIDTH, POOL_GROUP_WIDTH), POOL_GROUP_WIDTH),
        "pool_scale": 1.0 + 0.02 * jax.random.normal(ks[4], (DEPTH, POOL_WIDTH), f32),
        "w_branch_a": n(ks[5], (DEPTH, ATTN_WIDTH, D_MODEL), ATTN_WIDTH),
        "w_branch_b": n(ks[6], (DEPTH, POOL_WIDTH, D_MODEL), POOL_WIDTH),
        "w_out": n(ks[7], (DEPTH, D_MODEL, D_MODEL), D_MODEL),
        "norm_ffn_g": 1.0 + 0.02 * jax.random.normal(ks[8], (DEPTH, D_MODEL), f32),
        "w_up": n(ks[9], (DEPTH, D_MODEL, 2 * D_FF), D_MODEL),
        "conv_w": n(ks[10], (DEPTH, CONV_WIDTH, 2 * D_FF), CONV_WIDTH),
        "conv_b": 0.02 * jax.random.normal(ks[11], (DEPTH, 2 * D_FF), f32),
        "w_down": n(ks[12], (DEPTH, D_FF, D_MODEL), D_FF),
        "norm_final_g": 1.0 + 0.02 * jax.random.normal(ks[13], (D_MODEL,), f32),
    }


def reference(x, norm_mix_g, w_in, w_pool, pool_scale, w_branch_a, w_branch_b, w_out,
              norm_ffn_g, w_up, conv_w, conv_b, w_down, norm_final_g):
    B, S, _ = x.shape
    splits = [ATTN_WIDTH, 2 * ATTN_WIDTH, 3 * ATTN_WIDTH, 3 * ATTN_WIDTH + POOL_WIDTH]
    for layer in range(DEPTH):
        h = rms_norm(x, norm_mix_g[layer])
        proj = h @ w_in[layer]
        q, k, v, u, gates = jnp.split(proj, splits, axis=-1)
        hs = (B, S, ATTN_HEADS, HEAD_DIM)
        y_a = moba_attention(q.reshape(hs), k.reshape(hs), v.reshape(hs)) @ w_branch_a[layer]
        y_b = multiscale_pool(u, w_pool[layer], pool_scale[layer]) @ w_branch_b[layer]
        g = jax.nn.sigmoid(gates.astype(jnp.float32)).astype(x.dtype).reshape(B, S, N_BRANCH, D_MODEL)
        merged = g[:, :, 0] * y_a + g[:, :, 1] * y_b
        x = x + merged @ w_out[layer]
        h = rms_norm(x, norm_ffn_g[layer])
        x = x + conv_gated_mlp(h, w_up[layer], conv_w[layer], conv_b[layer], w_down[layer])
    return rms_norm(x, norm_final_g)
```

```python
import functools

import jax
import jax.numpy as jnp
from jax import lax
from jax.experimental import pallas as pl
from jax.experimental.pallas import tpu as pltpu

HEAD_DIM = 64
MOBA_BLOCK = 256
MOBA_TOPK = 3
POOL_WINDOWS = (2, 4, 8, 16)
CONV_WIDTH = 3
RMS_EPS = 1e-6
NEG_INF = -1e30

ROW_TILE = 512
POOL_HALO = 16
CONV_HALO = 8
FF_CHUNK = 256
V7X_VMEM_LIMIT = 56 * 1024 * 1024

_NT = (((1,), (1,)), ((), ()))


def _rms_norm(x, g):
    y = x * lax.rsqrt(jnp.mean(x * x, axis=-1, keepdims=True) + RMS_EPS)
    return y * g


def _in_proj_kernel(x_ref, g_ref, wq_ref, wk_ref, wvt_ref, wu_ref, q_ref, k_ref, vt_ref, u_ref, *, scale):
    h = _rms_norm(x_ref[0], g_ref[...]).astype(jnp.bfloat16)
    q_ref[0] = (jnp.dot(h, wq_ref[...], preferred_element_type=jnp.float32) * scale).astype(q_ref.dtype)
    k_ref[0] = jnp.dot(h, wk_ref[...], preferred_element_type=jnp.float32).astype(k_ref.dtype)
    u_ref[0] = jnp.dot(h, wu_ref[...], preferred_element_type=jnp.float32).astype(u_ref.dtype)
    vt = lax.dot_general(wvt_ref[...], h, _NT, preferred_element_type=jnp.float32)
    for j in range(vt_ref.shape[1]):
        vt_ref[0, j] = vt[:, j * MOBA_BLOCK:(j + 1) * MOBA_BLOCK].astype(vt_ref.dtype)


def _in_proj(x, g, wq, wk, wvt, wu):
    B, S, D = x.shape
    A = wq.shape[1]
    P = wu.shape[1]
    T = min(ROW_TILE, S)
    nb_t = T // MOBA_BLOCK
    nb = S // MOBA_BLOCK
    const = lambda b, s: (0, 0)
    return pl.pallas_call(
        functools.partial(_in_proj_kernel, scale=HEAD_DIM ** -0.5),
        grid=(B, S // T),
        in_specs=[
            pl.BlockSpec((1, T, D), lambda b, s: (b, s, 0)),
            pl.BlockSpec((1, D), const),
            pl.BlockSpec((D, A), const),
            pl.BlockSpec((D, A), const),
            pl.BlockSpec((A, D), const),
            pl.BlockSpec((D, P), const),
        ],
        out_specs=[
            pl.BlockSpec((1, T, A), lambda b, s: (b, s, 0)),
            pl.BlockSpec((1, T, A), lambda b, s: (b, s, 0)),
            pl.BlockSpec((1, nb_t, A, MOBA_BLOCK), lambda b, s: (b, s, 0, 0)),
            pl.BlockSpec((1, T, P), lambda b, s: (b, s, 0)),
        ],
        out_shape=[
            jax.ShapeDtypeStruct((B, S, A), jnp.bfloat16),
            jax.ShapeDtypeStruct((B, S, A), jnp.bfloat16),
            jax.ShapeDtypeStruct((B, nb, A, MOBA_BLOCK), jnp.bfloat16),
            jax.ShapeDtypeStruct((B, S, P), jnp.bfloat16),
        ],
        compiler_params=pltpu.CompilerParams(
            dimension_semantics=("parallel", "parallel"), vmem_limit_bytes=V7X_VMEM_LIMIT),
        name="in_proj",
    )(x, g, wq, wk, wvt, wu)


def _moba_kernel(q_ref, k_ref, vt_ref, o_ref, kmean_ref, sel_ref, ot_ref, *, n_heads, n_blocks):
    L = MOBA_BLOCK
    i = pl.program_id(1)

    @pl.when(i == 0)
    def _():
        row = lax.broadcasted_iota(jnp.int32, (n_blocks, n_blocks * L), 0)
        col = lax.broadcasted_iota(jnp.int32, (n_blocks, n_blocks * L), 1)
        ind = jnp.where((col >= row * L) & (col < (row + 1) * L), 1.0 / L, 0.0).astype(jnp.bfloat16)
        k_all = k_ref[0].reshape(n_blocks * L, k_ref.shape[-1])
        kmean_ref[...] = jnp.dot(ind, k_all, preferred_element_type=jnp.float32)

    blk = lax.broadcasted_iota(jnp.int32, (n_blocks, L), 0)
    key_pos = lax.broadcasted_iota(jnp.int32, (L, L), 0)
    qry_pos = lax.broadcasted_iota(jnp.int32, (L, L), 1)

    for h in range(n_heads):
        hs = slice(h * HEAD_DIM, (h + 1) * HEAD_DIM)
        q_h = q_ref[0, :, hs]

        km = kmean_ref[:, hs]
        km_hi = km.astype(jnp.bfloat16)
        km_lo = (km - km_hi.astype(jnp.float32)).astype(jnp.bfloat16)
        g2 = lax.dot_general(jnp.concatenate([km_hi, km_lo], axis=0), q_h, _NT,
                             preferred_element_type=jnp.float32)
        gate = g2[:n_blocks] + g2[n_blocks:]
        rank = jnp.zeros((n_blocks, L), jnp.float32)
        for j in range(n_blocks):
            gj = gate[j:j + 1, :]
            beats = (gj > gate) | ((gj == gate) & (blk > j))
            rank = rank + jnp.where(beats, (i > j).astype(jnp.float32), 0.0)
        sel_ref[...] = jnp.where((blk < i) & (rank < MOBA_TOPK), 1.0, 0.0)

        s = lax.dot_general(k_ref[0, i, :, hs], q_h, _NT, preferred_element_type=jnp.float32)
        s = jnp.where(key_pos <= qry_pos, s, NEG_INF)
        m0 = jnp.max(s, axis=0, keepdims=True)
        p = jnp.exp(s - m0)
        l0 = jnp.sum(p, axis=0, keepdims=True)
        acc0 = jnp.dot(vt_ref[0, i, hs, :], p.astype(jnp.bfloat16), preferred_element_type=jnp.float32)

        def past_block(b, carry):
            m, l, acc = carry
            s = lax.dot_general(k_ref[0, b, :, hs], q_h, _NT, preferred_element_type=jnp.float32)
            mb = jnp.max(s, axis=0, keepdims=True)
            on = sel_ref[pl.ds(b, 1), :] > 0.0
            m_new = jnp.where(on, jnp.maximum(m, mb), m)
            p = jnp.exp(s - jnp.where(on, m_new, mb))
            alpha = jnp.exp(m - m_new)
            ps = jnp.sum(p, axis=0, keepdims=True)
            pv = jnp.dot(vt_ref[0, b, hs, :], p.astype(jnp.bfloat16), preferred_element_type=jnp.float32)
            l = alpha * l + jnp.where(on, ps, 0.0)
            acc = alpha * acc + jnp.where(on, pv, 0.0)
            return m_new, l, acc

        _, l, acc = lax.fori_loop(0, i, past_block, (m0, l0, acc0))
        ot_ref[hs, :] = (acc / l).astype(ot_ref.dtype)

    eye = jnp.where(key_pos == qry_pos, 1.0, 0.0).astype(jnp.bfloat16)
    o_ref[0] = lax.dot_general(eye, ot_ref[...], _NT, preferred_element_type=jnp.float32).astype(o_ref.dtype)


def _moba(q, k, vt):
    B, S, A = q.shape
    L = MOBA_BLOCK
    nb = S // L
    k4 = k.reshape(B, nb, L, A)
    return pl.pallas_call(
        functools.partial(_moba_kernel, n_heads=A // HEAD_DIM, n_blocks=nb),
        grid=(B, nb),
        in_specs=[
            pl.BlockSpec((1, L, A), lambda b, i: (b, i, 0)),
            pl.BlockSpec((1, nb, L, A), lambda b, i: (b, 0, 0, 0)),
            pl.BlockSpec((1, nb, A, L), lambda b, i: (b, 0, 0, 0)),
        ],
        out_specs=pl.BlockSpec((1, L, A), lambda b, i: (b, i, 0)),
        out_shape=jax.ShapeDtypeStruct((B, S, A), jnp.bfloat16),
        scratch_shapes=[
            pltpu.VMEM((nb, A), jnp.float32),
            pltpu.VMEM((nb, L), jnp.float32),
            pltpu.VMEM((A, L), jnp.bfloat16),
        ],
        compiler_params=pltpu.CompilerParams(
            dimension_semantics=("parallel", "arbitrary"), vmem_limit_bytes=V7X_VMEM_LIMIT),
        name="moba",
    )(q, k4, vt)


def _merge_kernel(x_ref, a_ref, u_ref, g_ref, wg_ref, wa_ref, wp_ref, ps_ref, wb_ref, wo_ref,
                  o_ref, uext_ref):
    T = x_ref.shape[1]
    D = x_ref.shape[2]
    GW = wp_ref.shape[1]
    s_idx = pl.program_id(1)

    @pl.when(s_idx == 0)
    def _():
        uext_ref[0:POOL_HALO, :] = jnp.zeros((POOL_HALO, uext_ref.shape[1]), jnp.float32)

    x = x_ref[0]
    h = _rms_norm(x, g_ref[...]).astype(jnp.bfloat16)
    y_a = jnp.dot(a_ref[0], wa_ref[...], preferred_element_type=jnp.float32)

    u = u_ref[0].astype(jnp.float32)
    uext_ref[POOL_HALO:POOL_HALO + T, :] = u
    t_pos = s_idx * T + lax.broadcasted_iota(jnp.int32, (T, GW), 0)
    ys = []
    for gi, win in enumerate(POOL_WINDOWS):
        cs = slice(gi * GW, (gi + 1) * GW)
        wsum = u[:, cs]
        for d in range(1, win):
            wsum = wsum + uext_ref[POOL_HALO - d:POOL_HALO - d + T, cs]
        count = jnp.minimum(t_pos + 1, win).astype(jnp.float32)
        mixed = (wsum / count - u[:, cs]).astype(jnp.bfloat16)
        ys.append(jnp.dot(mixed, wp_ref[gi], preferred_element_type=jnp.float32))
    uext_ref[0:POOL_HALO, :] = uext_ref[T:T + POOL_HALO, :]
    pooled = (jnp.concatenate(ys, axis=1) * ps_ref[...]).astype(jnp.bfloat16)
    y_b = jnp.dot(pooled, wb_ref[...], preferred_element_type=jnp.float32)

    gates = jax.nn.sigmoid(jnp.dot(h, wg_ref[...], preferred_element_type=jnp.float32))
    merged = (gates[:, :D] * y_a + gates[:, D:] * y_b).astype(jnp.bfloat16)
    o_ref[0] = x + jnp.dot(merged, wo_ref[...], preferred_element_type=jnp.float32)


def _merge(x, a, u, g, wg, wa, wp, ps, wb, wo):
    B, S, D = x.shape
    A = a.shape[2]
    P = u.shape[2]
    T = min(ROW_TILE, S)
    row = lambda b, s: (b, s, 0)
    const2 = lambda b, s: (0, 0)
    return pl.pallas_call(
        _merge_kernel,
        grid=(B, S // T),
        in_specs=[
            pl.BlockSpec((1, T, D), row),
            pl.BlockSpec((1, T, A), row),
            pl.BlockSpec((1, T, P), row),
            pl.BlockSpec((1, D), const2),
            pl.BlockSpec(wg.shape, const2),
            pl.BlockSpec(wa.shape, const2),
            pl.BlockSpec(wp.shape, lambda b, s: (0, 0, 0)),
            pl.BlockSpec((1, P), const2),
            pl.BlockSpec(wb.shape, const2),
            pl.BlockSpec(wo.shape, const2),
        ],
        out_specs=pl.BlockSpec((1, T, D), row),
        out_shape=jax.ShapeDtypeStruct((B, S, D), x.dtype),
        scratch_shapes=[pltpu.VMEM((POOL_HALO + T, P), jnp.float32)],
        compiler_params=pltpu.CompilerParams(
            dimension_semantics=("parallel", "arbitrary"), vmem_limit_bytes=V7X_VMEM_LIMIT),
        name="merge",
    )(x, a, u, g, wg, wa, wp, ps, wb, wo)


def _ffn_kernel(x_ref, g_ref, wup_ref, cw_ref, cb_ref, wdn_ref, gf_ref, o_ref,
                ext_ref, carry_ref, act_ref, *, final_norm):
    T = x_ref.shape[1]
    F = wdn_ref.shape[0]
    FC = FF_CHUNK
    H = CONV_HALO
    s_idx = pl.program_id(1)

    @pl.when(s_idx == 0)
    def _():
        carry_ref[...] = jnp.zeros(carry_ref.shape, jnp.float32)

    x = x_ref[0]
    h = _rms_norm(x, g_ref[...]).astype(jnp.bfloat16)

    def conv_cols(c0):
        a = jnp.dot(h, wup_ref[:, c0:c0 + FC], preferred_element_type=jnp.float32)
        ext_ref[0:H, :] = carry_ref[:, c0:c0 + FC]
        ext_ref[H:H + T, :] = a
        carry_ref[:, c0:c0 + FC] = a[T - H:, :]
        w = cw_ref[:, c0:c0 + FC]
        return (cb_ref[:, c0:c0 + FC] + w[2:3, :] * a
                + w[1:2, :] * ext_ref[H - 1:H - 1 + T, :]
                + w[0:1, :] * ext_ref[H - 2:H - 2 + T, :])

    for c in range(F // FC):
        gate = conv_cols(c * FC)
        val = conv_cols(F + c * FC)
        act_ref[:, c * FC:(c + 1) * FC] = (gate * jax.nn.sigmoid(gate) * val).astype(act_ref.dtype)

    y = x + jnp.dot(act_ref[...], wdn_ref[...], preferred_element_type=jnp.float32)
    if final_norm:
        y = _rms_norm(y, gf_ref[...])
    o_ref[0] = y


def _ffn(x, g, wup, cw, cb, wdn, gf, final_norm):
    B, S, D = x.shape
    F = wdn.shape[0]
    T = min(ROW_TILE, S)
    row = lambda b, s: (b, s, 0)
    const2 = lambda b, s: (0, 0)
    return pl.pallas_call(
        functools.partial(_ffn_kernel, final_norm=final_norm),
        grid=(B, S // T),
        in_specs=[
            pl.BlockSpec((1, T, D), row),
            pl.BlockSpec((1, D), const2),
            pl.BlockSpec(wup.shape, const2),
            pl.BlockSpec(cw.shape, const2),
            pl.BlockSpec(cb.shape, const2),
            pl.BlockSpec(wdn.shape, const2),
            pl.BlockSpec((1, D), const2),
        ],
        out_specs=pl.BlockSpec((1, T, D), row),
        out_shape=jax.ShapeDtypeStruct((B, S, D), x.dtype),
        scratch_shapes=[
            pltpu.VMEM((CONV_HALO + T, FF_CHUNK), jnp.float32),
            pltpu.VMEM((CONV_HALO, 2 * F), jnp.float32),
            pltpu.VMEM((T, F), jnp.bfloat16),
        ],
        compiler_params=pltpu.CompilerParams(
            dimension_semantics=("parallel", "arbitrary"), vmem_limit_bytes=V7X_VMEM_LIMIT),
        name="ffn",
    )(x, g, wup, cw, cb, wdn, gf)


def kernel(x, norm_mix_g, w_in, w_pool, pool_scale, w_branch_a, w_branch_b, w_out, norm_ffn_g, w_up, conv_w, conv_b, w_down, norm_final_g):
    depth = w_in.shape[0]
    D = x.shape[-1]
    A = w_branch_a.shape[1]
    P = w_branch_b.shape[1]
    assert x.shape[1] % MOBA_BLOCK == 0 and A % HEAD_DIM == 0
    assert w_down.shape[1] % FF_CHUNK == 0 and POOL_WINDOWS[-1] <= POOL_HALO and CONV_WIDTH - 1 <= CONV_HALO
    bf = jnp.bfloat16
    gf = norm_final_g.reshape(1, D)
    for layer in range(depth):
        wi = w_in[layer].astype(bf)
        g_mix = norm_mix_g[layer].reshape(1, D)
        q, k, vt, u = _in_proj(x, g_mix, wi[:, :A], wi[:, A:2 * A], wi[:, 2 * A:3 * A].T, wi[:, 3 * A:3 * A + P])
        a = _moba(q, k, vt)
        x = _merge(x, a, u, g_mix, wi[:, 3 * A + P:], w_branch_a[layer].astype(bf), w_pool[layer].astype(bf),
                   pool_scale[layer].reshape(1, P), w_branch_b[layer].astype(bf), w_out[layer].astype(bf))
        x = _ffn(x, norm_ffn_g[layer].reshape(1, D), w_up[layer].astype(bf), conv_w[layer],
                 conv_b[layer].reshape(1, -1), w_down[layer].astype(bf), gf, layer == depth - 1)
    return x
```

```python
import functools

import jax
import jax.numpy as jnp
from jax import lax
from jax.experimental import pallas as pl
from jax.experimental.pallas import tpu as pltpu

HEAD_DIM = 64
MOBA_BLOCK = 256
MOBA_TOPK = 3
POOL_WINDOWS = (2, 4, 8, 16)
CONV_WIDTH = 3
RMS_EPS = 1e-6
NEG_INF = -1e30
LOG2_E = 1.4426950408889634

ROW_TILE = 512
POOL_HALO = 16
ONES_ROWS = 16
CONV_HALO = 8
FF_CHUNK = 256
V7X_VMEM_LIMIT = 56 * 1024 * 1024

_NT = (((1,), (1,)), ((), ()))


def _rms_norm(x, g):
    y = x * lax.rsqrt(jnp.mean(x * x, axis=-1, keepdims=True) + RMS_EPS)
    return y * g


def _in_proj_kernel(x_ref, g_ref, wq_ref, wk_ref, wvt_ref, wu_ref, q_ref, k_ref, vt_ref, u_ref, *, scale):
    h = _rms_norm(x_ref[0], g_ref[...]).astype(jnp.bfloat16)
    q_ref[0] = (jnp.dot(h, wq_ref[...], preferred_element_type=jnp.float32) * scale).astype(q_ref.dtype)
    k_ref[0] = jnp.dot(h, wk_ref[...], preferred_element_type=jnp.float32).astype(k_ref.dtype)
    u_ref[0] = jnp.dot(h, wu_ref[...], preferred_element_type=jnp.float32).astype(u_ref.dtype)
    vt = lax.dot_general(wvt_ref[...], h, _NT, preferred_element_type=jnp.float32)
    for j in range(vt_ref.shape[1]):
        vt_ref[0, j] = vt[:, j * MOBA_BLOCK:(j + 1) * MOBA_BLOCK].astype(vt_ref.dtype)


def _in_proj(x, g, wq, wk, wvt, wu):
    B, S, D = x.shape
    A = wq.shape[1]
    P = wu.shape[1]
    T = min(ROW_TILE, S)
    nb_t = T // MOBA_BLOCK
    nb = S // MOBA_BLOCK
    const = lambda b, s: (0, 0)
    return pl.pallas_call(
        functools.partial(_in_proj_kernel, scale=HEAD_DIM ** -0.5 * LOG2_E),
        grid=(B, S // T),
        in_specs=[
            pl.BlockSpec((1, T, D), lambda b, s: (b, s, 0)),
            pl.BlockSpec((1, D), const),
            pl.BlockSpec((D, A), const),
            pl.BlockSpec((D, A), const),
            pl.BlockSpec((A, D), const),
            pl.BlockSpec((D, P), const),
        ],
        out_specs=[
            pl.BlockSpec((1, T, A), lambda b, s: (b, s, 0)),
            pl.BlockSpec((1, T, A), lambda b, s: (b, s, 0)),
            pl.BlockSpec((1, nb_t, A, MOBA_BLOCK), lambda b, s: (b, s, 0, 0)),
            pl.BlockSpec((1, T, P), lambda b, s: (b, s, 0)),
        ],
        out_shape=[
            jax.ShapeDtypeStruct((B, S, A), jnp.bfloat16),
            jax.ShapeDtypeStruct((B, S, A), jnp.bfloat16),
            jax.ShapeDtypeStruct((B, nb, A, MOBA_BLOCK), jnp.bfloat16),
            jax.ShapeDtypeStruct((B, S, P), jnp.bfloat16),
        ],
        compiler_params=pltpu.CompilerParams(
            dimension_semantics=("parallel", "parallel"), vmem_limit_bytes=V7X_VMEM_LIMIT),
        name="in_proj",
    )(x, g, wq, wk, wvt, wu)


def _moba_kernel(q_ref, k_ref, vt_ref, o_ref, kmean_ref, qz_ref, sel_ref, m_ref, acc_ref, s_ref, ot_ref,
                 *, n_heads, n_blocks):
    L = MOBA_BLOCK
    G = 2 * HEAD_DIM
    i = pl.program_id(1)

    @pl.when(i == 0)
    def _():
        row = lax.broadcasted_iota(jnp.int32, (n_blocks, n_blocks * L), 0)
        col = lax.broadcasted_iota(jnp.int32, (n_blocks, n_blocks * L), 1)
        ind = jnp.where((col >= row * L) & (col < (row + 1) * L), 1.0 / L, 0.0).astype(jnp.bfloat16)
        k_all = k_ref[0].reshape(n_blocks * L, k_ref.shape[-1])
        kmean_ref[...] = jnp.dot(ind, k_all, preferred_element_type=jnp.float32)

    blk = lax.broadcasted_iota(jnp.int32, (n_blocks, L), 0)
    key_pos = lax.broadcasted_iota(jnp.int32, (L, L), 0)
    qry_pos = lax.broadcasted_iota(jnp.int32, (L, L), 1)
    lane = lax.broadcasted_iota(jnp.int32, (L, G), 1)
    ones_rows = jnp.ones((ONES_ROWS, L), jnp.bfloat16)
    groups = [slice((h // 2) * G, (h // 2 + 1) * G) for h in range(n_heads)]

    def scores(b, h):
        return lax.dot_general(k_ref[0, b, :, groups[h]], qz_ref[h], _NT, preferred_element_type=jnp.float32)

    def pv(b, h, p):
        lhs = jnp.concatenate([vt_ref[0, b, h * HEAD_DIM:(h + 1) * HEAD_DIM, :], ones_rows], axis=0)
        return jnp.dot(lhs, p.astype(jnp.bfloat16), preferred_element_type=jnp.float32)

    for h in range(n_heads):
        in_head = (lane >= (h % 2) * HEAD_DIM) & (lane < (h % 2 + 1) * HEAD_DIM)
        qz_ref[h] = jnp.where(in_head, q_ref[0, :, groups[h]], jnp.zeros((L, G), q_ref.dtype))
        km = kmean_ref[:, groups[h]]
        km_hi = km.astype(jnp.bfloat16)
        km_lo = (km - km_hi.astype(jnp.float32)).astype(jnp.bfloat16)
        g2 = lax.dot_general(jnp.concatenate([km_hi, km_lo], axis=0), qz_ref[h], _NT,
                             preferred_element_type=jnp.float32)
        gate = g2[:n_blocks] + g2[n_blocks:]
        rank = jnp.zeros((n_blocks, L), jnp.float32)
        for j in range(n_blocks):
            gj = gate[j:j + 1, :]
            beats = (gj > gate) | ((gj == gate) & (blk > j))
            rank = rank + jnp.where(beats, (i > j).astype(jnp.float32), 0.0)
        sel_ref[h] = jnp.where((blk < i) & (rank < MOBA_TOPK), 1.0, 0.0)

    for h in range(n_heads):
        s = jnp.where(key_pos <= qry_pos, scores(i, h), NEG_INF)
        s_ref[h] = s
        m_ref[h:h + 1, :] = jnp.max(s, axis=0, keepdims=True)
    for h in range(n_heads):
        acc_ref[h] = pv(i, h, jnp.exp2(s_ref[h] - m_ref[h:h + 1, :]))

    def past_block(b, carry):
        mbs = []
        for h in range(n_heads):
            s = scores(b, h)
            s_ref[h] = s
            mbs.append(jnp.max(s, axis=0, keepdims=True))
        new_state = []
        for h in range(n_heads):
            mb = mbs[h]
            on = sel_ref[h, pl.ds(b, 1), :] > 0.0
            m = m_ref[h:h + 1, :]
            m_new = jnp.where(on, jnp.maximum(m, mb), m)
            p = jnp.exp2(s_ref[h] - jnp.where(on, m_new, mb))
            alpha = jnp.exp2(m - m_new)
            new_state.append((m_new, alpha * acc_ref[h] + jnp.where(on, pv(b, h, p), 0.0)))
        for h in range(n_heads):
            m_ref[h:h + 1, :], acc_ref[h] = new_state[h]
        return carry

    lax.fori_loop(0, i, past_block, 0)

    for h in range(n_heads):
        inv_l = 1.0 / acc_ref[h, HEAD_DIM:HEAD_DIM + 1, :]
        ot_ref[h * HEAD_DIM:(h + 1) * HEAD_DIM, :] = (acc_ref[h, 0:HEAD_DIM, :] * inv_l).astype(ot_ref.dtype)
    eye = jnp.where(key_pos == qry_pos, 1.0, 0.0).astype(jnp.bfloat16)
    o_ref[0] = lax.dot_general(eye, ot_ref[...], _NT, preferred_element_type=jnp.float32).astype(o_ref.dtype)


def _moba(q, k, vt):
    B, S, A = q.shape
    L = MOBA_BLOCK
    nb = S // L
    H = A // HEAD_DIM
    k4 = k.reshape(B, nb, L, A)
    return pl.pallas_call(
        functools.partial(_moba_kernel, n_heads=H, n_blocks=nb),
        grid=(B, nb),
        in_specs=[
            pl.BlockSpec((1, L, A), lambda b, i: (b, i, 0)),
            pl.BlockSpec((1, nb, L, A), lambda b, i: (b, 0, 0, 0)),
            pl.BlockSpec((1, nb, A, L), lambda b, i: (b, 0, 0, 0)),
        ],
        out_specs=pl.BlockSpec((1, L, A), lambda b, i: (b, i, 0)),
        out_shape=jax.ShapeDtypeStruct((B, S, A), jnp.bfloat16),
        scratch_shapes=[
            pltpu.VMEM((nb, A), jnp.float32),
            pltpu.VMEM((H, L, 2 * HEAD_DIM), jnp.bfloat16),
            pltpu.VMEM((H, nb, L), jnp.float32),
            pltpu.VMEM((H, L), jnp.float32),
            pltpu.VMEM((H, HEAD_DIM + ONES_ROWS, L), jnp.float32),
            pltpu.VMEM((H, L, L), jnp.float32),
            pltpu.VMEM((A, L), jnp.bfloat16),
        ],
        compiler_params=pltpu.CompilerParams(
            dimension_semantics=("parallel", "arbitrary"), vmem_limit_bytes=V7X_VMEM_LIMIT),
        name="moba",
    )(q, k4, vt)


def _merge_kernel(x_ref, a_ref, u_ref, g_ref, wg_ref, wa_ref, wp_ref, ps_ref, wb_ref, wo_ref,
                  o_ref, uext_ref):
    T = x_ref.shape[1]
    D = x_ref.shape[2]
    GW = wp_ref.shape[1]
    s_idx = pl.program_id(1)

    @pl.when(s_idx == 0)
    def _():
        uext_ref[0:POOL_HALO, :] = jnp.zeros((POOL_HALO, uext_ref.shape[1]), jnp.float32)

    x = x_ref[0]
    h = _rms_norm(x, g_ref[...]).astype(jnp.bfloat16)
    y_a = jnp.dot(a_ref[0], wa_ref[...], preferred_element_type=jnp.float32)

    u = u_ref[0].astype(jnp.float32)
    uext_ref[POOL_HALO:POOL_HALO + T, :] = u
    t_pos = s_idx * T + lax.broadcasted_iota(jnp.int32, (T, GW), 0)
    ys = []
    for gi, win in enumerate(POOL_WINDOWS):
        cs = slice(gi * GW, (gi + 1) * GW)
        wsum = u[:, cs]
        for d in range(1, win):
            wsum = wsum + uext_ref[POOL_HALO - d:POOL_HALO - d + T, cs]
        count = jnp.minimum(t_pos + 1, win).astype(jnp.float32)
        mixed = (wsum / count - u[:, cs]).astype(jnp.bfloat16)
        ys.append(jnp.dot(mixed, wp_ref[gi], preferred_element_type=jnp.float32))
    uext_ref[0:POOL_HALO, :] = uext_ref[T:T + POOL_HALO, :]
    pooled = (jnp.concatenate(ys, axis=1) * ps_ref[...]).astype(jnp.bfloat16)
    y_b = jnp.dot(pooled, wb_ref[...], preferred_element_type=jnp.float32)

    gates = jax.nn.sigmoid(jnp.dot(h, wg_ref[...], preferred_element_type=jnp.float32))
    merged = (gates[:, :D] * y_a + gates[:, D:] * y_b).astype(jnp.bfloat16)
    o_ref[0] = x + jnp.dot(merged, wo_ref[...], preferred_element_type=jnp.float32)


def _merge(x, a, u, g, wg, wa, wp, ps, wb, wo):
    B, S, D = x.shape
    A = a.shape[2]
    P = u.shape[2]
    T = min(ROW_TILE, S)
    row = lambda b, s: (b, s, 0)
    const2 = lambda b, s: (0, 0)
    return pl.pallas_call(
        _merge_kernel,
        grid=(B, S // T),
        in_specs=[
            pl.BlockSpec((1, T, D), row),
            pl.BlockSpec((1, T, A), row),
            pl.BlockSpec((1, T, P), row),
            pl.BlockSpec((1, D), const2),
            pl.BlockSpec(wg.shape, const2),
            pl.BlockSpec(wa.shape, const2),
            pl.BlockSpec(wp.shape, lambda b, s: (0, 0, 0)),
            pl.BlockSpec((1, P), const2),
            pl.BlockSpec(wb.shape, const2),
            pl.BlockSpec(wo.shape, const2),
        ],
        out_specs=pl.BlockSpec((1, T, D), row),
        out_shape=jax.ShapeDtypeStruct((B, S, D), x.dtype),
        scratch_shapes=[pltpu.VMEM((POOL_HALO + T, P), jnp.float32)],
        compiler_params=pltpu.CompilerParams(
            dimension_semantics=("parallel", "arbitrary"), vmem_limit_bytes=V7X_VMEM_LIMIT),
        name="merge",
    )(x, a, u, g, wg, wa, wp, ps, wb, wo)


def _ffn_kernel(x_ref, g_ref, wup_ref, cw_ref, cb_ref, wdn_ref, gf_ref, o_ref,
                ext_ref, carry_ref, act_ref, *, final_norm):
    T = x_ref.shape[1]
    F = wdn_ref.shape[0]
    FC = FF_CHUNK
    H = CONV_HALO
    s_idx = pl.program_id(1)

    @pl.when(s_idx == 0)
    def _():
        carry_ref[...] = jnp.zeros(carry_ref.shape, jnp.float32)

    x = x_ref[0]
    h = _rms_norm(x, g_ref[...]).astype(jnp.bfloat16)

    def conv_cols(c0):
        a = jnp.dot(h, wup_ref[:, c0:c0 + FC], preferred_element_type=jnp.float32)
        ext_ref[0:H, :] = carry_ref[:, c0:c0 + FC]
        ext_ref[H:H + T, :] = a
        carry_ref[:, c0:c0 + FC] = a[T - H:, :]
        w = cw_ref[:, c0:c0 + FC]
        return (cb_ref[:, c0:c0 + FC] + w[2:3, :] * a
                + w[1:2, :] * ext_ref[H - 1:H - 1 + T, :]
                + w[0:1, :] * ext_ref[H - 2:H - 2 + T, :])

    for c in range(F // FC):
        gate = conv_cols(c * FC)
        val = conv_cols(F + c * FC)
        act_ref[:, c * FC:(c + 1) * FC] = (gate * jax.nn.sigmoid(gate) * val).astype(act_ref.dtype)

    y = x + jnp.dot(act_ref[...], wdn_ref[...], preferred_element_type=jnp.float32)
    if final_norm:
        y = _rms_norm(y, gf_ref[...])
    o_ref[0] = y


def _ffn(x, g, wup, cw, cb, wdn, gf, final_norm):
    B, S, D = x.shape
    F = wdn.shape[0]
    T = min(ROW_TILE, S)
    row = lambda b, s: (b, s, 0)
    const2 = lambda b, s: (0, 0)
    return pl.pallas_call(
        functools.partial(_ffn_kernel, final_norm=final_norm),
        grid=(B, S // T),
        in_specs=[
            pl.BlockSpec((1, T, D), row),
            pl.BlockSpec((1, D), const2),
            pl.BlockSpec(wup.shape, const2),
            pl.BlockSpec(cw.shape, const2),
            pl.BlockSpec(cb.shape, const2),
            pl.BlockSpec(wdn.shape, const2),
            pl.BlockSpec((1, D), const2),
        ],
        out_specs=pl.BlockSpec((1, T, D), row),
        out_shape=jax.ShapeDtypeStruct((B, S, D), x.dtype),
        scratch_shapes=[
            pltpu.VMEM((CONV_HALO + T, FF_CHUNK), jnp.float32),
            pltpu.VMEM((CONV_HALO, 2 * F), jnp.float32),
            pltpu.VMEM((T, F), jnp.bfloat16),
        ],
        compiler_params=pltpu.CompilerParams(
            dimension_semantics=("parallel", "arbitrary"), vmem_limit_bytes=V7X_VMEM_LIMIT),
        name="ffn",
    )(x, g, wup, cw, cb, wdn, gf)


def kernel(x, norm_mix_g, w_in, w_pool, pool_scale, w_branch_a, w_branch_b, w_out, norm_ffn_g, w_up, conv_w, conv_b, w_down, norm_final_g):
    depth = w_in.shape[0]
    D = x.shape[-1]
    A = w_branch_a.shape[1]
    P = w_branch_b.shape[1]
    assert x.shape[1] % MOBA_BLOCK == 0 and A % (2 * HEAD_DIM) == 0
    assert w_down.shape[1] % FF_CHUNK == 0 and POOL_WINDOWS[-1] <= POOL_HALO and CONV_WIDTH - 1 <= CONV_HALO
    bf = jnp.bfloat16
    gf = norm_final_g.reshape(1, D)
    for layer in range(depth):
        wi = w_in[layer].astype(bf)
        g_mix = norm_mix_g[layer].reshape(1, D)
        q, k, vt, u = _in_proj(x, g_mix, wi[:, :A], wi[:, A:2 * A], wi[:, 2 * A:3 * A].T, wi[:, 3 * A:3 * A + P])
        a = _moba(q, k, vt)
        x = _merge(x, a, u, g_mix, wi[:, 3 * A + P:], w_branch_a[layer].astype(bf), w_pool[layer].astype(bf),
                   pool_scale[layer].reshape(1, P), w_branch_b[layer].astype(bf), w_out[layer].astype(bf))
        x = _ffn(x, norm_ffn_g[layer].reshape(1, D), w_up[layer].astype(bf), conv_w[layer],
                 conv_b[layer].reshape(1, -1), w_down[layer].astype(bf), gf, layer == depth - 1)
    return x
```

```python
import functools

import jax
import jax.numpy as jnp
from jax import lax
from jax.experimental import pallas as pl
from jax.experimental.pallas import tpu as pltpu

HEAD_DIM = 64
MOBA_BLOCK = 256
MOBA_TOPK = 3
POOL_WINDOWS = (2, 4, 8, 16)
CONV_WIDTH = 3
RMS_EPS = 1e-6
NEG_INF = -1e30
LOG2_E = 1.4426950408889634

ROW_TILE = 512
POOL_HALO = 16
ONES_ROWS = 16
CONV_HALO = 8
FF_CHUNK = 256
V7X_VMEM_LIMIT = 56 * 1024 * 1024

_NT = (((1,), (1,)), ((), ()))


def _rms_norm(x, g):
    y = x * lax.rsqrt(jnp.mean(x * x, axis=-1, keepdims=True) + RMS_EPS)
    return y * g


def _in_proj_kernel(x_ref, g_ref, wq_ref, wk_ref, wvt_ref, wu_ref, q_ref, k_ref, vt_ref, u_ref, *, scale):
    h = _rms_norm(x_ref[0], g_ref[...]).astype(jnp.bfloat16)
    q_ref[0] = (jnp.dot(h, wq_ref[...], preferred_element_type=jnp.float32) * scale).astype(q_ref.dtype)
    k_ref[0] = jnp.dot(h, wk_ref[...], preferred_element_type=jnp.float32).astype(k_ref.dtype)
    u_ref[0] = jnp.dot(h, wu_ref[...], preferred_element_type=jnp.float32).astype(u_ref.dtype)
    vt = lax.dot_general(wvt_ref[...], h, _NT, preferred_element_type=jnp.float32)
    for j in range(vt_ref.shape[1]):
        vt_ref[0, j] = vt[:, j * MOBA_BLOCK:(j + 1) * MOBA_BLOCK].astype(vt_ref.dtype)


def _in_proj(x, g, wq, wk, wvt, wu):
    B, S, D = x.shape
    A = wq.shape[1]
    P = wu.shape[1]
    T = min(ROW_TILE, S)
    nb_t = T // MOBA_BLOCK
    nb = S // MOBA_BLOCK
    const = lambda b, s: (0, 0)
    return pl.pallas_call(
        functools.partial(_in_proj_kernel, scale=HEAD_DIM ** -0.5 * LOG2_E),
        grid=(B, S // T),
        in_specs=[
            pl.BlockSpec((1, T, D), lambda b, s: (b, s, 0)),
            pl.BlockSpec((1, D), const),
            pl.BlockSpec((D, A), const),
            pl.BlockSpec((D, A), const),
            pl.BlockSpec((A, D), const),
            pl.BlockSpec((D, P), const),
        ],
        out_specs=[
            pl.BlockSpec((1, T, A), lambda b, s: (b, s, 0)),
            pl.BlockSpec((1, T, A), lambda b, s: (b, s, 0)),
            pl.BlockSpec((1, nb_t, A, MOBA_BLOCK), lambda b, s: (b, s, 0, 0)),
            pl.BlockSpec((1, T, P), lambda b, s: (b, s, 0)),
        ],
        out_shape=[
            jax.ShapeDtypeStruct((B, S, A), jnp.bfloat16),
            jax.ShapeDtypeStruct((B, S, A), jnp.bfloat16),
            jax.ShapeDtypeStruct((B, nb, A, MOBA_BLOCK), jnp.bfloat16),
            jax.ShapeDtypeStruct((B, S, P), jnp.bfloat16),
        ],
        compiler_params=pltpu.CompilerParams(
            dimension_semantics=("parallel", "parallel"), vmem_limit_bytes=V7X_VMEM_LIMIT),
        name="in_proj",
    )(x, g, wq, wk, wvt, wu)


def _moba_kernel(q_ref, k_ref, vt_ref, o_ref, kmean_ref, qz_ref, sel_ref, m_ref, acc_ref,
                 s0_ref, mb0_ref, s1_ref, mb1_ref, ot_ref, *, n_heads, n_blocks):
    L = MOBA_BLOCK
    G = 2 * HEAD_DIM
    i = pl.program_id(1)

    @pl.when(i == 0)
    def _():
        row = lax.broadcasted_iota(jnp.int32, (n_blocks, n_blocks * L), 0)
        col = lax.broadcasted_iota(jnp.int32, (n_blocks, n_blocks * L), 1)
        ind = jnp.where((col >= row * L) & (col < (row + 1) * L), 1.0 / L, 0.0).astype(jnp.bfloat16)
        k_all = k_ref[0].reshape(n_blocks * L, k_ref.shape[-1])
        kmean_ref[...] = jnp.dot(ind, k_all, preferred_element_type=jnp.float32)

    blk = lax.broadcasted_iota(jnp.int32, (n_blocks, L), 0)
    key_pos = lax.broadcasted_iota(jnp.int32, (L, L), 0)
    qry_pos = lax.broadcasted_iota(jnp.int32, (L, L), 1)
    lane = lax.broadcasted_iota(jnp.int32, (L, G), 1)
    ones_rows = jnp.ones((ONES_ROWS, L), jnp.bfloat16)
    groups = [slice((h // 2) * G, (h // 2 + 1) * G) for h in range(n_heads)]

    def scores(b, h):
        return lax.dot_general(k_ref[0, b, :, groups[h]], qz_ref[h], _NT, preferred_element_type=jnp.float32)

    def pv(b, h, p):
        lhs = jnp.concatenate([vt_ref[0, b, h * HEAD_DIM:(h + 1) * HEAD_DIM, :], ones_rows], axis=0)
        return jnp.dot(lhs, p.astype(jnp.bfloat16), preferred_element_type=jnp.float32)

    for h in range(n_heads):
        in_head = (lane >= (h % 2) * HEAD_DIM) & (lane < (h % 2 + 1) * HEAD_DIM)
        qz_ref[h] = jnp.where(in_head, q_ref[0, :, groups[h]], jnp.zeros((L, G), q_ref.dtype))
        km = kmean_ref[:, groups[h]]
        km_hi = km.astype(jnp.bfloat16)
        km_lo = (km - km_hi.astype(jnp.float32)).astype(jnp.bfloat16)
        g2 = lax.dot_general(jnp.concatenate([km_hi, km_lo], axis=0), qz_ref[h], _NT,
                             preferred_element_type=jnp.float32)
        gate = g2[:n_blocks] + g2[n_blocks:]
        rank = jnp.zeros((n_blocks, L), jnp.float32)
        for j in range(n_blocks):
            gj = gate[j:j + 1, :]
            beats = (gj > gate) | ((gj == gate) & (blk > j))
            rank = rank + jnp.where(beats, (i > j).astype(jnp.float32), 0.0)
        sel_ref[h] = jnp.where((blk < i) & (rank < MOBA_TOPK), 1.0, 0.0)

    def stage_a(b, h, slot, causal):
        s_ref, mb_ref = slot
        s = scores(b, h)
        if causal:
            s = jnp.where(key_pos <= qry_pos, s, NEG_INF)
        s_ref[h] = s
        mb_ref[h:h + 1, :] = jnp.max(s, axis=0, keepdims=True)

    def stage_b(t, h, slot):
        s_ref, mb_ref = slot
        is_own = t == 0
        b = jnp.where(is_own, i, t - 1)
        mb = mb_ref[h:h + 1, :]
        on = jnp.where(is_own, 1.0, sel_ref[h, pl.ds(jnp.maximum(t - 1, 0), 1), :]) > 0.0
        m = m_ref[h:h + 1, :]
        m_new = jnp.where(on, jnp.maximum(m, mb), m)
        p = jnp.exp2(s_ref[h] - jnp.where(on, m_new, mb))
        alpha = jnp.exp2(m - m_new)
        return m_new, alpha * acc_ref[h] + jnp.where(on, pv(b, h, p), 0.0)

    def store_state(new_state):
        for h in range(n_heads):
            m_ref[h:h + 1, :], acc_ref[h] = new_state[h]

    def pipeline_step(t, cur, nxt):
        new_state = []
        for h in range(n_heads):
            stage_a(t, h, nxt, causal=False)
            new_state.append(stage_b(t, h, cur))
        store_state(new_state)

    def drain(cur):
        store_state([stage_b(i, h, cur) for h in range(n_heads)])

    slot0, slot1 = (s0_ref, mb0_ref), (s1_ref, mb1_ref)
    m_ref[...] = jnp.full(m_ref.shape, NEG_INF, jnp.float32)
    acc_ref[...] = jnp.zeros(acc_ref.shape, jnp.float32)
    for h in range(n_heads):
        stage_a(i, h, slot0, causal=True)

    def step_pair(j, carry):
        pipeline_step(2 * j, slot0, slot1)
        pipeline_step(2 * j + 1, slot1, slot0)
        return carry

    lax.fori_loop(0, i // 2, step_pair, 0)

    @pl.when(i % 2 == 1)
    def _():
        pipeline_step(i - 1, slot0, slot1)
        drain(slot1)

    @pl.when(i % 2 == 0)
    def _():
        drain(slot0)

    for h in range(n_heads):
        inv_l = 1.0 / acc_ref[h, HEAD_DIM:HEAD_DIM + 1, :]
        ot_ref[h * HEAD_DIM:(h + 1) * HEAD_DIM, :] = (acc_ref[h, 0:HEAD_DIM, :] * inv_l).astype(ot_ref.dtype)
    eye = jnp.where(key_pos == qry_pos, 1.0, 0.0).astype(jnp.bfloat16)
    o_ref[0] = lax.dot_general(eye, ot_ref[...], _NT, preferred_element_type=jnp.float32).astype(o_ref.dtype)


def _moba(q, k, vt):
    B, S, A = q.shape
    L = MOBA_BLOCK
    nb = S // L
    H = A // HEAD_DIM
    k4 = k.reshape(B, nb, L, A)
    return pl.pallas_call(
        functools.partial(_moba_kernel, n_heads=H, n_blocks=nb),
        grid=(B, nb),
        in_specs=[
            pl.BlockSpec((1, L, A), lambda b, i: (b, i, 0)),
            pl.BlockSpec((1, nb, L, A), lambda b, i: (b, 0, 0, 0)),
            pl.BlockSpec((1, nb, A, L), lambda b, i: (b, 0, 0, 0)),
        ],
        out_specs=pl.BlockSpec((1, L, A), lambda b, i: (b, i, 0)),
        out_shape=jax.ShapeDtypeStruct((B, S, A), jnp.bfloat16),
        scratch_shapes=[
            pltpu.VMEM((nb, A), jnp.float32),
            pltpu.VMEM((H, L, 2 * HEAD_DIM), jnp.bfloat16),
            pltpu.VMEM((H, nb, L), jnp.float32),
            pltpu.VMEM((H, L), jnp.float32),
            pltpu.VMEM((H, HEAD_DIM + ONES_ROWS, L), jnp.float32),
            pltpu.VMEM((H, L, L), jnp.float32),
            pltpu.VMEM((H, L), jnp.float32),
            pltpu.VMEM((H, L, L), jnp.float32),
            pltpu.VMEM((H, L), jnp.float32),
            pltpu.VMEM((A, L), jnp.bfloat16),
        ],
        compiler_params=pltpu.CompilerParams(
            dimension_semantics=("parallel", "arbitrary"), vmem_limit_bytes=V7X_VMEM_LIMIT),
        name="moba",
    )(q, k4, vt)


def _merge_kernel(x_ref, a_ref, u_ref, g_ref, wg_ref, wa_ref, wp_ref, ps_ref, wb_ref, wo_ref,
                  o_ref, uext_ref):
    T = x_ref.shape[1]
    D = x_ref.shape[2]
    GW = wp_ref.shape[1]
    s_idx = pl.program_id(1)

    @pl.when(s_idx == 0)
    def _():
        uext_ref[0:POOL_HALO, :] = jnp.zeros((POOL_HALO, uext_ref.shape[1]), jnp.float32)

    x = x_ref[0]
    h = _rms_norm(x, g_ref[...]).astype(jnp.bfloat16)
    y_a = jnp.dot(a_ref[0], wa_ref[...], preferred_element_type=jnp.float32)

    u = u_ref[0].astype(jnp.float32)
    uext_ref[POOL_HALO:POOL_HALO + T, :] = u
    t_pos = s_idx * T + lax.broadcasted_iota(jnp.int32, (T, GW), 0)
    ys = []
    for gi, win in enumerate(POOL_WINDOWS):
        cs = slice(gi * GW, (gi + 1) * GW)
        wsum = u[:, cs]
        for d in range(1, win):
            wsum = wsum + uext_ref[POOL_HALO - d:POOL_HALO - d + T, cs]
        count = jnp.minimum(t_pos + 1, win).astype(jnp.float32)
        mixed = (wsum / count - u[:, cs]).astype(jnp.bfloat16)
        ys.append(jnp.dot(mixed, wp_ref[gi], preferred_element_type=jnp.float32))
    uext_ref[0:POOL_HALO, :] = uext_ref[T:T + POOL_HALO, :]
    pooled = (jnp.concatenate(ys, axis=1) * ps_ref[...]).astype(jnp.bfloat16)
    y_b = jnp.dot(pooled, wb_ref[...], preferred_element_type=jnp.float32)

    gates = jax.nn.sigmoid(jnp.dot(h, wg_ref[...], preferred_element_type=jnp.float32))
    merged = (gates[:, :D] * y_a + gates[:, D:] * y_b).astype(jnp.bfloat16)
    o_ref[0] = x + jnp.dot(merged, wo_ref[...], preferred_element_type=jnp.float32)


def _merge(x, a, u, g, wg, wa, wp, ps, wb, wo):
    B, S, D = x.shape
    A = a.shape[2]
    P = u.shape[2]
    T = min(ROW_TILE, S)
    row = lambda b, s: (b, s, 0)
    const2 = lambda b, s: (0, 0)
    return pl.pallas_call(
        _merge_kernel,
        grid=(B, S // T),
        in_specs=[
            pl.BlockSpec((1, T, D), row),
            pl.BlockSpec((1, T, A), row),
            pl.BlockSpec((1, T, P), row),
            pl.BlockSpec((1, D), const2),
            pl.BlockSpec(wg.shape, const2),
            pl.BlockSpec(wa.shape, const2),
            pl.BlockSpec(wp.shape, lambda b, s: (0, 0, 0)),
            pl.BlockSpec((1, P), const2),
            pl.BlockSpec(wb.shape, const2),
            pl.BlockSpec(wo.shape, const2),
        ],
        out_specs=pl.BlockSpec((1, T, D), row),
        out_shape=jax.ShapeDtypeStruct((B, S, D), x.dtype),
        scratch_shapes=[pltpu.VMEM((POOL_HALO + T, P), jnp.float32)],
        compiler_params=pltpu.CompilerParams(
            dimension_semantics=("parallel", "arbitrary"), vmem_limit_bytes=V7X_VMEM_LIMIT),
        name="merge",
    )(x, a, u, g, wg, wa, wp, ps, wb, wo)


def _ffn_kernel(x_ref, g_ref, wup_ref, cw_ref, cb_ref, wdn_ref, gf_ref, o_ref,
                ext_ref, carry_ref, act_ref, *, final_norm):
    T = x_ref.shape[1]
    F = wdn_ref.shape[0]
    FC = FF_CHUNK
    H = CONV_HALO
    s_idx = pl.program_id(1)

    @pl.when(s_idx == 0)
    def _():
        carry_ref[...] = jnp.zeros(carry_ref.shape, jnp.float32)

    x = x_ref[0]
    h = _rms_norm(x, g_ref[...]).astype(jnp.bfloat16)

    def conv_cols(c0):
        a = jnp.dot(h, wup_ref[:, c0:c0 + FC], preferred_element_type=jnp.float32)
        ext_ref[0:H, :] = carry_ref[:, c0:c0 + FC]
        ext_ref[H:H + T, :] = a
        carry_ref[:, c0:c0 + FC] = a[T - H:, :]
        w = cw_ref[:, c0:c0 + FC]
        return (cb_ref[:, c0:c0 + FC] + w[2:3, :] * a
                + w[1:2, :] * ext_ref[H - 1:H - 1 + T, :]
                + w[0:1, :] * ext_ref[H - 2:H - 2 + T, :])

    for c in range(F // FC):
        gate = conv_cols(c * FC)
        val = conv_cols(F + c * FC)
        act_ref[:, c * FC:(c + 1) * FC] = (gate * jax.nn.sigmoid(gate) * val).astype(act_ref.dtype)

    y = x + jnp.dot(act_ref[...], wdn_ref[...], preferred_element_type=jnp.float32)
    if final_norm:
        y = _rms_norm(y, gf_ref[...])
    o_ref[0] = y


def _ffn(x, g, wup, cw, cb, wdn, gf, final_norm):
    B, S, D = x.shape
    F = wdn.shape[0]
    T = min(ROW_TILE, S)
    row = lambda b, s: (b, s, 0)
    const2 = lambda b, s: (0, 0)
    return pl.pallas_call(
        functools.partial(_ffn_kernel, final_norm=final_norm),
        grid=(B, S // T),
        in_specs=[
            pl.BlockSpec((1, T, D), row),
            pl.BlockSpec((1, D), const2),
            pl.BlockSpec(wup.shape, const2),
            pl.BlockSpec(cw.shape, const2),
            pl.BlockSpec(cb.shape, const2),
            pl.BlockSpec(wdn.shape, const2),
            pl.BlockSpec((1, D), const2),
        ],
        out_specs=pl.BlockSpec((1, T, D), row),
        out_shape=jax.ShapeDtypeStruct((B, S, D), x.dtype),
        scratch_shapes=[
            pltpu.VMEM((CONV_HALO + T, FF_CHUNK), jnp.float32),
            pltpu.VMEM((CONV_HALO, 2 * F), jnp.float32),
            pltpu.VMEM((T, F), jnp.bfloat16),
        ],
        compiler_params=pltpu.CompilerParams(
            dimension_semantics=("parallel", "arbitrary"), vmem_limit_bytes=V7X_VMEM_LIMIT),
        name="ffn",
    )(x, g, wup, cw, cb, wdn, gf)


def kernel(x, norm_mix_g, w_in, w_pool, pool_scale, w_branch_a, w_branch_b, w_out, norm_ffn_g, w_up, conv_w, conv_b, w_down, norm_final_g):
    depth = w_in.shape[0]
    D = x.shape[-1]
    A = w_branch_a.shape[1]
    P = w_branch_b.shape[1]
    assert x.shape[1] % MOBA_BLOCK == 0 and A % (2 * HEAD_DIM) == 0
    assert w_down.shape[1] % FF_CHUNK == 0 and POOL_WINDOWS[-1] <= POOL_HALO and CONV_WIDTH - 1 <= CONV_HALO
    bf = jnp.bfloat16
    gf = norm_final_g.reshape(1, D)
    for layer in range(depth):
        wi = w_in[layer].astype(bf)
        g_mix = norm_mix_g[layer].reshape(1, D)
        q, k, vt, u = _in_proj(x, g_mix, wi[:, :A], wi[:, A:2 * A], wi[:, 2 * A:3 * A].T, wi[:, 3 * A:3 * A + P])
        a = _moba(q, k, vt)
        x = _merge(x, a, u, g_mix, wi[:, 3 * A + P:], w_branch_a[layer].astype(bf), w_pool[layer].astype(bf),
                   pool_scale[layer].reshape(1, P), w_branch_b[layer].astype(bf), w_out[layer].astype(bf))
        x = _ffn(x, norm_ffn_g[layer].reshape(1, D), w_up[layer].astype(bf), conv_w[layer],
                 conv_b[layer].reshape(1, -1), w_down[layer].astype(bf), gf, layer == depth - 1)
    return x
```

```python
import functools

import jax
import jax.numpy as jnp
from jax import lax
from jax.experimental import pallas as pl
from jax.experimental.pallas import tpu as pltpu

HEAD_DIM = 64
MOBA_BLOCK = 256
MOBA_TOPK = 3
POOL_WINDOWS = (2, 4, 8, 16)
CONV_WIDTH = 3
RMS_EPS = 1e-6
NEG_INF = -1e30
LOG2_E = 1.4426950408889634

ROW_TILE = 1024
FFN_ROW_TILE = 512
SUB_TILE = 256
POOL_HALO = 32
ONES_ROWS = 16
CONV_HALO = 8
FF_CHUNK = 256
V7X_VMEM_LIMIT = 56 * 1024 * 1024

_NT = (((1,), (1,)), ((), ()))


def _resident(shape):
    return pl.BlockSpec(shape, lambda b, s: (0,) * len(shape), pipeline_mode=pl.Buffered(1))


def _rms_norm(x, g):
    y = x * lax.rsqrt(jnp.mean(x * x, axis=-1, keepdims=True) + RMS_EPS)
    return y * g


def _in_proj_kernel(x_ref, g_ref, wq_ref, wk_ref, wvt_ref, wu_ref, q_ref, k_ref, vt_ref, u_ref, *, scale):
    for j in range(vt_ref.shape[1]):
        rows = slice(j * MOBA_BLOCK, (j + 1) * MOBA_BLOCK)
        h = _rms_norm(x_ref[0, rows, :], g_ref[...]).astype(jnp.bfloat16)
        q_ref[0, rows, :] = (jnp.dot(h, wq_ref[...], preferred_element_type=jnp.float32) * scale).astype(q_ref.dtype)
        k_ref[0, rows, :] = jnp.dot(h, wk_ref[...], preferred_element_type=jnp.float32).astype(k_ref.dtype)
        u_ref[0, rows, :] = jnp.dot(h, wu_ref[...], preferred_element_type=jnp.float32).astype(u_ref.dtype)
        vt_ref[0, j] = lax.dot_general(wvt_ref[...], h, _NT,
                                       preferred_element_type=jnp.float32).astype(vt_ref.dtype)


def _in_proj(x, g, wq, wk, wvt, wu):
    B, S, D = x.shape
    A = wq.shape[1]
    P = wu.shape[1]
    T = min(ROW_TILE, S)
    assert S % T == 0 and T % SUB_TILE == 0
    nb_t = T // MOBA_BLOCK
    nb = S // MOBA_BLOCK
    return pl.pallas_call(
        functools.partial(_in_proj_kernel, scale=HEAD_DIM ** -0.5 * LOG2_E),
        grid=(B, S // T),
        in_specs=[
            pl.BlockSpec((1, T, D), lambda b, s: (b, s, 0)),
            _resident((1, D)),
            _resident((D, A)),
            _resident((D, A)),
            _resident((A, D)),
            _resident((D, P)),
        ],
        out_specs=[
            pl.BlockSpec((1, T, A), lambda b, s: (b, s, 0)),
            pl.BlockSpec((1, T, A), lambda b, s: (b, s, 0)),
            pl.BlockSpec((1, nb_t, A, MOBA_BLOCK), lambda b, s: (b, s, 0, 0)),
            pl.BlockSpec((1, T, P), lambda b, s: (b, s, 0)),
        ],
        out_shape=[
            jax.ShapeDtypeStruct((B, S, A), jnp.bfloat16),
            jax.ShapeDtypeStruct((B, S, A), jnp.bfloat16),
            jax.ShapeDtypeStruct((B, nb, A, MOBA_BLOCK), jnp.bfloat16),
            jax.ShapeDtypeStruct((B, S, P), jnp.bfloat16),
        ],
        compiler_params=pltpu.CompilerParams(
            dimension_semantics=("parallel", "parallel"), vmem_limit_bytes=V7X_VMEM_LIMIT),
        name="in_proj",
    )(x, g, wq, wk, wvt, wu)


def _moba_kernel(q_ref, k_ref, vt_ref, o_ref, kmean_ref, qz_ref, sel_ref, m_ref, acc_ref,
                 s0_ref, mb0_ref, s1_ref, mb1_ref, ot_ref, *, n_heads, n_blocks):
    L = MOBA_BLOCK
    G = 2 * HEAD_DIM
    i = pl.program_id(1)

    @pl.when(i == 0)
    def _():
        row = lax.broadcasted_iota(jnp.int32, (n_blocks, n_blocks * L), 0)
        col = lax.broadcasted_iota(jnp.int32, (n_blocks, n_blocks * L), 1)
        ind = jnp.where((col >= row * L) & (col < (row + 1) * L), 1.0 / L, 0.0).astype(jnp.bfloat16)
        k_all = k_ref[0].reshape(n_blocks * L, k_ref.shape[-1])
        kmean_ref[...] = jnp.dot(ind, k_all, preferred_element_type=jnp.float32)

    blk = lax.broadcasted_iota(jnp.int32, (n_blocks, L), 0)
    key_pos = lax.broadcasted_iota(jnp.int32, (L, L), 0)
    qry_pos = lax.broadcasted_iota(jnp.int32, (L, L), 1)
    lane = lax.broadcasted_iota(jnp.int32, (L, G), 1)
    ones_rows = jnp.ones((ONES_ROWS, L), jnp.bfloat16)
    groups = [slice((h // 2) * G, (h // 2 + 1) * G) for h in range(n_heads)]

    def scores(b, h):
        return lax.dot_general(k_ref[0, b, :, groups[h]], qz_ref[h], _NT, preferred_element_type=jnp.float32)

    def pv(b, h, p):
        lhs = jnp.concatenate([vt_ref[0, b, h * HEAD_DIM:(h + 1) * HEAD_DIM, :], ones_rows], axis=0)
        return jnp.dot(lhs, p.astype(jnp.bfloat16), preferred_element_type=jnp.float32)

    for h in range(n_heads):
        in_head = (lane >= (h % 2) * HEAD_DIM) & (lane < (h % 2 + 1) * HEAD_DIM)
        qz_ref[h] = jnp.where(in_head, q_ref[0, :, groups[h]], jnp.zeros((L, G), q_ref.dtype))
        km = kmean_ref[:, groups[h]]
        km_hi = km.astype(jnp.bfloat16)
        km_lo = (km - km_hi.astype(jnp.float32)).astype(jnp.bfloat16)
        g2 = lax.dot_general(jnp.concatenate([km_hi, km_lo], axis=0), qz_ref[h], _NT,
                             preferred_element_type=jnp.float32)
        gate = g2[:n_blocks] + g2[n_blocks:]
        rank = jnp.zeros((n_blocks, L), jnp.float32)
        for j in range(n_blocks):
            gj = gate[j:j + 1, :]
            beats = (gj > gate) | ((gj == gate) & (blk > j))
            rank = rank + jnp.where(beats, (i > j).astype(jnp.float32), 0.0)
        sel_ref[h] = jnp.where((blk < i) & (rank < MOBA_TOPK), 1.0, 0.0)

    def stage_a(b, h, slot, causal):
        s_ref, mb_ref = slot
        s = scores(b, h)
        if causal:
            s = jnp.where(key_pos <= qry_pos, s, NEG_INF)
        s_ref[h] = s
        mb_ref[h:h + 1, :] = jnp.max(s, axis=0, keepdims=True)

    def stage_b(t, h, slot):
        s_ref, mb_ref = slot
        is_own = t == 0
        b = jnp.where(is_own, i, t - 1)
        mb = mb_ref[h:h + 1, :]
        on = jnp.where(is_own, 1.0, sel_ref[h, pl.ds(jnp.maximum(t - 1, 0), 1), :]) > 0.0
        m = m_ref[h:h + 1, :]
        m_new = jnp.where(on, jnp.maximum(m, mb), m)
        p = jnp.exp2(s_ref[h] - jnp.where(on, m_new, mb))
        alpha = jnp.exp2(m - m_new)
        return m_new, alpha * acc_ref[h] + jnp.where(on, pv(b, h, p), 0.0)

    def store_state(new_state):
        for h in range(n_heads):
            m_ref[h:h + 1, :], acc_ref[h] = new_state[h]

    def pipeline_step(t, cur, nxt):
        new_state = []
        for h in range(n_heads):
            stage_a(t, h, nxt, causal=False)
            new_state.append(stage_b(t, h, cur))
        store_state(new_state)

    def drain(cur):
        store_state([stage_b(i, h, cur) for h in range(n_heads)])

    slot0, slot1 = (s0_ref, mb0_ref), (s1_ref, mb1_ref)
    m_ref[...] = jnp.full(m_ref.shape, NEG_INF, jnp.float32)
    acc_ref[...] = jnp.zeros(acc_ref.shape, jnp.float32)
    for h in range(n_heads):
        stage_a(i, h, slot0, causal=True)

    def step_pair(j, carry):
        pipeline_step(2 * j, slot0, slot1)
        pipeline_step(2 * j + 1, slot1, slot0)
        return carry

    lax.fori_loop(0, i // 2, step_pair, 0)

    @pl.when(i % 2 == 1)
    def _():
        pipeline_step(i - 1, slot0, slot1)
        drain(slot1)

    @pl.when(i % 2 == 0)
    def _():
        drain(slot0)

    for h in range(n_heads):
        inv_l = 1.0 / acc_ref[h, HEAD_DIM:HEAD_DIM + 1, :]
        ot_ref[h * HEAD_DIM:(h + 1) * HEAD_DIM, :] = (acc_ref[h, 0:HEAD_DIM, :] * inv_l).astype(ot_ref.dtype)
    eye = jnp.where(key_pos == qry_pos, 1.0, 0.0).astype(jnp.bfloat16)
    o_ref[0] = lax.dot_general(eye, ot_ref[...], _NT, preferred_element_type=jnp.float32).astype(o_ref.dtype)


def _moba(q, k, vt):
    B, S, A = q.shape
    L = MOBA_BLOCK
    nb = S // L
    H = A // HEAD_DIM
    k4 = k.reshape(B, nb, L, A)
    return pl.pallas_call(
        functools.partial(_moba_kernel, n_heads=H, n_blocks=nb),
        grid=(B, nb),
        in_specs=[
            pl.BlockSpec((1, L, A), lambda b, i: (b, i, 0)),
            pl.BlockSpec((1, nb, L, A), lambda b, i: (b, 0, 0, 0)),
            pl.BlockSpec((1, nb, A, L), lambda b, i: (b, 0, 0, 0)),
        ],
        out_specs=pl.BlockSpec((1, L, A), lambda b, i: (b, i, 0)),
        out_shape=jax.ShapeDtypeStruct((B, S, A), jnp.bfloat16),
        scratch_shapes=[
            pltpu.VMEM((nb, A), jnp.float32),
            pltpu.VMEM((H, L, 2 * HEAD_DIM), jnp.bfloat16),
            pltpu.VMEM((H, nb, L), jnp.float32),
            pltpu.VMEM((H, L), jnp.float32),
            pltpu.VMEM((H, HEAD_DIM + ONES_ROWS, L), jnp.float32),
            pltpu.VMEM((H, L, L), jnp.float32),
            pltpu.VMEM((H, L), jnp.float32),
            pltpu.VMEM((H, L, L), jnp.float32),
            pltpu.VMEM((H, L), jnp.float32),
            pltpu.VMEM((A, L), jnp.bfloat16),
        ],
        compiler_params=pltpu.CompilerParams(
            dimension_semantics=("parallel", "arbitrary"), vmem_limit_bytes=V7X_VMEM_LIMIT),
        name="moba",
    )(q, k4, vt)


def _sigmoid(x):
    return 0.5 * jnp.tanh(0.5 * x) + 0.5


def _merge_kernel(x_ref, a_ref, u_ref, g_ref, wg_ref, wa_ref, wp_ref, ps_ref, wb_ref, wo_ref,
                  o_ref, uext_ref, w2_ref, w4_ref, w8_ref):
    T = x_ref.shape[1]
    D = x_ref.shape[2]
    GW = wp_ref.shape[1]
    R = SUB_TILE
    HALO = POOL_HALO
    s_idx = pl.program_id(1)

    @pl.when(s_idx == 0)
    def _():
        uext_ref[0:HALO, :] = jnp.zeros((HALO, uext_ref.shape[1]), jnp.float32)

    def branch_in(r0):
        y_a = jnp.dot(a_ref[0, r0:r0 + R, :], wa_ref[...], preferred_element_type=jnp.float32)
        h = _rms_norm(x_ref[0, r0:r0 + R, :], g_ref[...]).astype(jnp.bfloat16)
        return y_a, jnp.dot(h, wg_ref[...], preferred_element_type=jnp.float32)

    def window_sums():
        n = HALO + T
        uext_ref[HALO:n, :] = u_ref[0].astype(jnp.float32)
        w2_ref[8:n, :] = uext_ref[8:n, :] + uext_ref[7:n - 1, :]
        w4_ref[16:n, GW:] = w2_ref[16:n, GW:] + w2_ref[14:n - 2, GW:]
        w8_ref[24:n, 2 * GW:] = w4_ref[24:n, 2 * GW:] + w4_ref[20:n - 4, 2 * GW:]

    def branch_out(r0, y_a, g_pre):
        e0 = HALO + r0
        wsums = (w2_ref[e0:e0 + R, 0:GW], w4_ref[e0:e0 + R, GW:2 * GW], w8_ref[e0:e0 + R, 2 * GW:3 * GW],
                 w8_ref[e0:e0 + R, 3 * GW:] + w8_ref[e0 - 8:e0 - 8 + R, 3 * GW:])
        t_pos = s_idx * T + r0 + lax.broadcasted_iota(jnp.int32, (R, GW), 0)
        ys = []
        for gi, win in enumerate(POOL_WINDOWS):
            count = jnp.minimum(t_pos + 1, win).astype(jnp.float32)
            mixed = (wsums[gi] / count - uext_ref[e0:e0 + R, gi * GW:(gi + 1) * GW]).astype(jnp.bfloat16)
            ys.append(jnp.dot(mixed, wp_ref[gi], preferred_element_type=jnp.float32))
        pooled = (jnp.concatenate(ys, axis=1) * ps_ref[...]).astype(jnp.bfloat16)
        y_b = jnp.dot(pooled, wb_ref[...], preferred_element_type=jnp.float32)
        gates = _sigmoid(g_pre)
        merged = (gates[:, :D] * y_a + gates[:, D:] * y_b).astype(jnp.bfloat16)
        o_ref[0, r0:r0 + R, :] = x_ref[0, r0:r0 + R, :] + jnp.dot(merged, wo_ref[...],
                                                                    preferred_element_type=jnp.float32)

    n_sub = T // R
    pending = branch_in(0)
    window_sums()
    for j in range(n_sub):
        nxt = branch_in((j + 1) * R) if j + 1 < n_sub else None
        branch_out(j * R, *pending)
        pending = nxt
    uext_ref[0:HALO, :] = uext_ref[T:T + HALO, :]


def _merge(x, a, u, g, wg, wa, wp, ps, wb, wo):
    B, S, D = x.shape
    A = a.shape[2]
    P = u.shape[2]
    T = min(ROW_TILE, S)
    assert S % T == 0 and T % SUB_TILE == 0
    row = lambda b, s: (b, s, 0)
    return pl.pallas_call(
        _merge_kernel,
        grid=(B, S // T),
        in_specs=[
            pl.BlockSpec((1, T, D), row),
            pl.BlockSpec((1, T, A), row),
            pl.BlockSpec((1, T, P), row),
            _resident((1, D)),
            _resident(wg.shape),
            _resident(wa.shape),
            _resident(wp.shape),
            _resident((1, P)),
            _resident(wb.shape),
            _resident(wo.shape),
        ],
        out_specs=pl.BlockSpec((1, T, D), row),
        out_shape=jax.ShapeDtypeStruct((B, S, D), x.dtype),
        scratch_shapes=[pltpu.VMEM((POOL_HALO + T, P), jnp.float32)] * 4,
        compiler_params=pltpu.CompilerParams(
            dimension_semantics=("parallel", "arbitrary"), vmem_limit_bytes=V7X_VMEM_LIMIT),
        name="merge",
    )(x, a, u, g, wg, wa, wp, ps, wb, wo)


def _ffn_kernel(x_ref, g_ref, wup_ref, cw_ref, cb_ref, wdn_ref, gf_ref, o_ref,
                ext_ref, carry_ref, act_ref, *, final_norm):
    T = x_ref.shape[1]
    F = wdn_ref.shape[0]
    FC = FF_CHUNK
    H = CONV_HALO
    s_idx = pl.program_id(1)

    @pl.when(s_idx == 0)
    def _():
        carry_ref[...] = jnp.zeros(carry_ref.shape, jnp.float32)

    h = _rms_norm(x_ref[0], g_ref[...]).astype(jnp.bfloat16)

    def conv_cols(c0, ext, scale):
        a = jnp.dot(h, wup_ref[:, c0:c0 + FC], preferred_element_type=jnp.float32)
        ext[0:H, :] = carry_ref[:, c0:c0 + FC]
        ext[H:H + T, :] = a
        carry_ref[:, c0:c0 + FC] = a[T - H:, :]
        w = cw_ref[:, c0:c0 + FC] * scale
        return (cb_ref[:, c0:c0 + FC] * scale + w[2:3, :] * a
                + w[1:2, :] * ext[H - 1:H - 1 + T, :]
                + w[0:1, :] * ext[H - 2:H - 2 + T, :])

    for c in range(F // FC):
        half_gate = conv_cols(c * FC, ext_ref.at[(2 * c) % ext_ref.shape[0]], 0.5)
        val = conv_cols(F + c * FC, ext_ref.at[(2 * c + 1) % ext_ref.shape[0]], 1.0)
        act_ref[:, c * FC:(c + 1) * FC] = (half_gate * (jnp.tanh(half_gate) + 1.0) * val).astype(act_ref.dtype)

    y = x_ref[0] + jnp.dot(act_ref[...], wdn_ref[...], preferred_element_type=jnp.float32)
    if final_norm:
        y = _rms_norm(y, gf_ref[...])
    o_ref[0] = y


def _ffn(x, g, wup, cw, cb, wdn, gf, final_norm):
    B, S, D = x.shape
    F = wdn.shape[0]
    T = min(FFN_ROW_TILE, S)
    assert S % T == 0
    row = lambda b, s: (b, s, 0)
    return pl.pallas_call(
        functools.partial(_ffn_kernel, final_norm=final_norm),
        grid=(B, S // T),
        in_specs=[
            pl.BlockSpec((1, T, D), row),
            _resident((1, D)),
            _resident(wup.shape),
            _resident(cw.shape),
            _resident(cb.shape),
            _resident(wdn.shape),
            _resident((1, D)),
        ],
        out_specs=pl.BlockSpec((1, T, D), row),
        out_shape=jax.ShapeDtypeStruct((B, S, D), x.dtype),
        scratch_shapes=[
            pltpu.VMEM((1, CONV_HALO + T, FF_CHUNK), jnp.float32),
            pltpu.VMEM((CONV_HALO, 2 * F), jnp.float32),
            pltpu.VMEM((T, F), jnp.bfloat16),
        ],
        compiler_params=pltpu.CompilerParams(
            dimension_semantics=("parallel", "arbitrary"), vmem_limit_bytes=V7X_VMEM_LIMIT),
        name="ffn",
    )(x, g, wup, cw, cb, wdn, gf)


def kernel(x, norm_mix_g, w_in, w_pool, pool_scale, w_branch_a, w_branch_b, w_out, norm_ffn_g, w_up, conv_w, conv_b, w_down, norm_final_g):
    depth = w_in.shape[0]
    D = x.shape[-1]
    A = w_branch_a.shape[1]
    P = w_branch_b.shape[1]
    assert x.shape[1] % MOBA_BLOCK == 0 and A % (2 * HEAD_DIM) == 0
    assert w_down.shape[1] % FF_CHUNK == 0 and 2 * POOL_WINDOWS[-1] <= POOL_HALO and POOL_WINDOWS == (2, 4, 8, 16) and CONV_WIDTH - 1 <= CONV_HALO
    bf = jnp.bfloat16
    gf = norm_final_g.reshape(1, D)
    for layer in range(depth):
        wi = w_in[layer].astype(bf)
        g_mix = norm_mix_g[layer].reshape(1, D)
        q, k, vt, u = _in_proj(x, g_mix, wi[:, :A], wi[:, A:2 * A], wi[:, 2 * A:3 * A].T, wi[:, 3 * A:3 * A + P])
        a = _moba(q, k, vt)
        x = _merge(x, a, u, g_mix, wi[:, 3 * A + P:], w_branch_a[layer].astype(bf), w_pool[layer].astype(bf),
                   pool_scale[layer].reshape(1, P), w_branch_b[layer].astype(bf), w_out[layer].astype(bf))
        x = _ffn(x, norm_ffn_g[layer].reshape(1, D), w_up[layer].astype(bf), conv_w[layer],
                 conv_b[layer].reshape(1, -1), w_down[layer].astype(bf), gf, layer == depth - 1)
    return x
```

```python
import functools

import jax
import jax.numpy as jnp
from jax import lax
from jax.experimental import pallas as pl
from jax.experimental.pallas import tpu as pltpu

HEAD_DIM = 64
MOBA_BLOCK = 256
MOBA_TOPK = 3
POOL_WINDOWS = (2, 4, 8, 16)
CONV_WIDTH = 3
RMS_EPS = 1e-6
NEG_INF = -1e30
LOG2_E = 1.4426950408889634

ROW_TILE = 1024
FFN_ROW_TILE = 512
SUB_TILE = 256
POOL_HALO = 32
ONES_ROWS = 16
CONV_HALO = 8
FF_CHUNK = 256
V7X_VMEM_LIMIT = 56 * 1024 * 1024

_NT = (((1,), (1,)), ((), ()))


def _layer_operand(block, layer, index=None):
    index = (0,) * len(block) if index is None else index
    return pl.BlockSpec((None,) + tuple(block), lambda b, s: (layer,) + tuple(index),
                        pipeline_mode=pl.Buffered(1))


def _rms_norm(x, g):
    y = x * lax.rsqrt(jnp.mean(x * x, axis=-1, keepdims=True) + RMS_EPS)
    return y * g


def _in_proj_kernel(x_ref, g_ref, wq_ref, wk_ref, wvt_ref, wu_ref, q_ref, k_ref, vt_ref, u_ref, *, scale):
    for j in range(vt_ref.shape[1]):
        rows = slice(j * MOBA_BLOCK, (j + 1) * MOBA_BLOCK)
        h = _rms_norm(x_ref[0, rows, :], g_ref[...]).astype(jnp.bfloat16)
        q_ref[0, rows, :] = (jnp.dot(h, wq_ref[...], preferred_element_type=jnp.float32) * scale).astype(q_ref.dtype)
        k_ref[0, rows, :] = jnp.dot(h, wk_ref[...], preferred_element_type=jnp.float32).astype(k_ref.dtype)
        u_ref[0, rows, :] = jnp.dot(h, wu_ref[...], preferred_element_type=jnp.float32).astype(u_ref.dtype)
        vt_ref[0, j] = lax.dot_general(wvt_ref[...], h, _NT,
                                       preferred_element_type=jnp.float32).astype(vt_ref.dtype)


def _in_proj(x, g, w_in, wvt, layer, A, P):
    B, S, D = x.shape
    assert P == A, "u is addressed as column block 3 of width A"
    T = min(ROW_TILE, S)
    assert S % T == 0 and T % SUB_TILE == 0
    nb_t = T // MOBA_BLOCK
    nb = S // MOBA_BLOCK
    return pl.pallas_call(
        functools.partial(_in_proj_kernel, scale=HEAD_DIM ** -0.5 * LOG2_E),
        grid=(B, S // T),
        in_specs=[
            pl.BlockSpec((1, T, D), lambda b, s: (b, s, 0)),
            _layer_operand((1, D), layer),
            _layer_operand((D, A), layer, (0, 0)),
            _layer_operand((D, A), layer, (0, 1)),
            _layer_operand((A, D), layer),
            _layer_operand((D, P), layer, (0, 3)),
        ],
        out_specs=[
            pl.BlockSpec((1, T, A), lambda b, s: (b, s, 0)),
            pl.BlockSpec((1, T, A), lambda b, s: (b, s, 0)),
            pl.BlockSpec((1, nb_t, A, MOBA_BLOCK), lambda b, s: (b, s, 0, 0)),
            pl.BlockSpec((1, T, P), lambda b, s: (b, s, 0)),
        ],
        out_shape=[
            jax.ShapeDtypeStruct((B, S, A), jnp.bfloat16),
            jax.ShapeDtypeStruct((B, S, A), jnp.bfloat16),
            jax.ShapeDtypeStruct((B, nb, A, MOBA_BLOCK), jnp.bfloat16),
            jax.ShapeDtypeStruct((B, S, P), jnp.bfloat16),
        ],
        compiler_params=pltpu.CompilerParams(
            dimension_semantics=("parallel", "parallel"), vmem_limit_bytes=V7X_VMEM_LIMIT),
        name="in_proj",
    )(x, g, w_in, w_in, wvt, w_in)


def _moba_kernel(q_ref, k_ref, vt_ref, o_ref, kmean_ref, qz_ref, sel_ref, m_ref, acc_ref,
                 s0_ref, mb0_ref, s1_ref, mb1_ref, ot_ref, *, n_heads, n_blocks):
    L = MOBA_BLOCK
    G = 2 * HEAD_DIM
    i = pl.program_id(1)

    @pl.when(i == 0)
    def _():
        row = lax.broadcasted_iota(jnp.int32, (n_blocks, n_blocks * L), 0)
        col = lax.broadcasted_iota(jnp.int32, (n_blocks, n_blocks * L), 1)
        ind = jnp.where((col >= row * L) & (col < (row + 1) * L), 1.0 / L, 0.0).astype(jnp.bfloat16)
        k_all = k_ref[0].reshape(n_blocks * L, k_ref.shape[-1])
        kmean_ref[...] = jnp.dot(ind, k_all, preferred_element_type=jnp.float32)

    blk = lax.broadcasted_iota(jnp.int32, (n_blocks, L), 0)
    key_pos = lax.broadcasted_iota(jnp.int32, (L, L), 0)
    qry_pos = lax.broadcasted_iota(jnp.int32, (L, L), 1)
    lane = lax.broadcasted_iota(jnp.int32, (L, G), 1)
    ones_rows = jnp.ones((ONES_ROWS, L), jnp.bfloat16)
    groups = [slice((h // 2) * G, (h // 2 + 1) * G) for h in range(n_heads)]

    def scores(b, h):
        return lax.dot_general(k_ref[0, b, :, groups[h]], qz_ref[h], _NT, preferred_element_type=jnp.float32)

    def pv(b, h, p):
        lhs = jnp.concatenate([vt_ref[0, b, h * HEAD_DIM:(h + 1) * HEAD_DIM, :], ones_rows], axis=0)
        return jnp.dot(lhs, p.astype(jnp.bfloat16), preferred_element_type=jnp.float32)

    for h in range(n_heads):
        in_head = (lane >= (h % 2) * HEAD_DIM) & (lane < (h % 2 + 1) * HEAD_DIM)
        qz_ref[h] = jnp.where(in_head, q_ref[0, :, groups[h]], jnp.zeros((L, G), q_ref.dtype))
        km = kmean_ref[:, groups[h]]
        km_hi = km.astype(jnp.bfloat16)
        km_lo = (km - km_hi.astype(jnp.float32)).astype(jnp.bfloat16)
        g2 = lax.dot_general(jnp.concatenate([km_hi, km_lo], axis=0), qz_ref[h], _NT,
                             preferred_element_type=jnp.float32)
        gate = g2[:n_blocks] + g2[n_blocks:]
        rank = jnp.zeros((n_blocks, L), jnp.float32)
        for j in range(n_blocks):
            gj = gate[j:j + 1, :]
            beats = (gj > gate) | ((gj == gate) & (blk > j))
            rank = rank + jnp.where(beats, (i > j).astype(jnp.float32), 0.0)
        sel_ref[h] = jnp.where((blk < i) & (rank < MOBA_TOPK), 1.0, 0.0)

    def stage_a(b, h, slot, causal):
        s_ref, mb_ref = slot
        s = scores(b, h)
        if causal:
            s = jnp.where(key_pos <= qry_pos, s, NEG_INF)
        s_ref[h] = s
        mb_ref[h:h + 1, :] = jnp.max(s, axis=0, keepdims=True)

    def stage_b(t, h, slot):
        s_ref, mb_ref = slot
        is_own = t == 0
        b = jnp.where(is_own, i, t - 1)
        mb = mb_ref[h:h + 1, :]
        on = jnp.where(is_own, 1.0, sel_ref[h, pl.ds(jnp.maximum(t - 1, 0), 1), :]) > 0.0
        m = m_ref[h:h + 1, :]
        m_new = jnp.where(on, jnp.maximum(m, mb), m)
        p = jnp.exp2(s_ref[h] - jnp.where(on, m_new, -NEG_INF))
        alpha = jnp.exp2(m - m_new)
        return m_new, alpha * acc_ref[h] + pv(b, h, p)

    def store_state(new_state):
        for h in range(n_heads):
            m_ref[h:h + 1, :], acc_ref[h] = new_state[h]

    def pipeline_step(t, cur, nxt):
        new_state = []
        for h in range(n_heads):
            stage_a(t, h, nxt, causal=False)
            new_state.append(stage_b(t, h, cur))
        store_state(new_state)

    def drain(cur):
        store_state([stage_b(i, h, cur) for h in range(n_heads)])

    slot0, slot1 = (s0_ref, mb0_ref), (s1_ref, mb1_ref)
    m_ref[...] = jnp.full(m_ref.shape, NEG_INF, jnp.float32)
    acc_ref[...] = jnp.zeros(acc_ref.shape, jnp.float32)
    for h in range(n_heads):
        stage_a(i, h, slot0, causal=True)

    def step_pair(j, carry):
        pipeline_step(2 * j, slot0, slot1)
        pipeline_step(2 * j + 1, slot1, slot0)
        return carry

    lax.fori_loop(0, i // 2, step_pair, 0)

    @pl.when(i % 2 == 1)
    def _():
        pipeline_step(i - 1, slot0, slot1)
        drain(slot1)

    @pl.when(i % 2 == 0)
    def _():
        drain(slot0)

    for h in range(n_heads):
        inv_l = 1.0 / acc_ref[h, HEAD_DIM:HEAD_DIM + 1, :]
        ot_ref[h * HEAD_DIM:(h + 1) * HEAD_DIM, :] = (acc_ref[h, 0:HEAD_DIM, :] * inv_l).astype(ot_ref.dtype)
    eye = jnp.where(key_pos == qry_pos, 1.0, 0.0).astype(jnp.bfloat16)
    o_ref[0] = lax.dot_general(eye, ot_ref[...], _NT, preferred_element_type=jnp.float32).astype(o_ref.dtype)


def _moba(q, k, vt):
    B, S, A = q.shape
    L = MOBA_BLOCK
    nb = S // L
    H = A // HEAD_DIM
    k4 = k.reshape(B, nb, L, A)
    return pl.pallas_call(
        functools.partial(_moba_kernel, n_heads=H, n_blocks=nb),
        grid=(B, nb),
        in_specs=[
            pl.BlockSpec((1, L, A), lambda b, i: (b, i, 0)),
            pl.BlockSpec((1, nb, L, A), lambda b, i: (b, 0, 0, 0)),
            pl.BlockSpec((1, nb, A, L), lambda b, i: (b, 0, 0, 0)),
        ],
        out_specs=pl.BlockSpec((1, L, A), lambda b, i: (b, i, 0)),
        out_shape=jax.ShapeDtypeStruct((B, S, A), jnp.bfloat16),
        scratch_shapes=[
            pltpu.VMEM((nb, A), jnp.float32),
            pltpu.VMEM((H, L, 2 * HEAD_DIM), jnp.bfloat16),
            pltpu.VMEM((H, nb, L), jnp.float32),
            pltpu.VMEM((H, L), jnp.float32),
            pltpu.VMEM((H, HEAD_DIM + ONES_ROWS, L), jnp.float32),
            pltpu.VMEM((H, L, L), jnp.float32),
            pltpu.VMEM((H, L), jnp.float32),
            pltpu.VMEM((H, L, L), jnp.float32),
            pltpu.VMEM((H, L), jnp.float32),
            pltpu.VMEM((A, L), jnp.bfloat16),
        ],
        compiler_params=pltpu.CompilerParams(
            dimension_semantics=("parallel", "arbitrary"), vmem_limit_bytes=V7X_VMEM_LIMIT),
        name="moba",
    )(q, k4, vt)


def _sigmoid(x):
    return 0.5 * jnp.tanh(0.5 * x) + 0.5


def _merge_kernel(x_ref, a_ref, u_ref, g_ref, wg_ref, wa_ref, wp_ref, ps_ref, wb_ref, wo_ref,
                  o_ref, uext_ref, w2_ref, w4_ref, w8_ref):
    T = x_ref.shape[1]
    D = x_ref.shape[2]
    GW = wp_ref.shape[1]
    R = SUB_TILE
    HALO = POOL_HALO
    s_idx = pl.program_id(1)

    @pl.when(s_idx == 0)
    def _():
        uext_ref[0:HALO, :] = jnp.zeros((HALO, uext_ref.shape[1]), jnp.float32)

    def branch_in(r0):
        y_a = jnp.dot(a_ref[0, r0:r0 + R, :], wa_ref[...], preferred_element_type=jnp.float32)
        h = _rms_norm(x_ref[0, r0:r0 + R, :], g_ref[...]).astype(jnp.bfloat16)
        return y_a, jnp.dot(h, wg_ref[...], preferred_element_type=jnp.float32)

    def window_sums():
        n = HALO + T
        uext_ref[HALO:n, :] = u_ref[0].astype(jnp.float32)
        w2_ref[8:n, :] = uext_ref[8:n, :] + uext_ref[7:n - 1, :]
        w4_ref[16:n, GW:] = w2_ref[16:n, GW:] + w2_ref[14:n - 2, GW:]
        w8_ref[24:n, 2 * GW:] = w4_ref[24:n, 2 * GW:] + w4_ref[20:n - 4, 2 * GW:]

    def branch_out(r0, y_a, g_pre):
        e0 = HALO + r0
        wsums = (w2_ref[e0:e0 + R, 0:GW], w4_ref[e0:e0 + R, GW:2 * GW], w8_ref[e0:e0 + R, 2 * GW:3 * GW],
                 w8_ref[e0:e0 + R, 3 * GW:] + w8_ref[e0 - 8:e0 - 8 + R, 3 * GW:])
        t_pos = s_idx * T + r0 + lax.broadcasted_iota(jnp.int32, (R, GW), 0)
        ys = []
        for gi, win in enumerate(POOL_WINDOWS):
            count = jnp.minimum(t_pos + 1, win).astype(jnp.float32)
            mixed = (wsums[gi] / count - uext_ref[e0:e0 + R, gi * GW:(gi + 1) * GW]).astype(jnp.bfloat16)
            ys.append(jnp.dot(mixed, wp_ref[gi], preferred_element_type=jnp.float32))
        pooled = (jnp.concatenate(ys, axis=1) * ps_ref[...]).astype(jnp.bfloat16)
        y_b = jnp.dot(pooled, wb_ref[...], preferred_element_type=jnp.float32)
        gates = _sigmoid(g_pre)
        merged = (gates[:, :D] * y_a + gates[:, D:] * y_b).astype(jnp.bfloat16)
        o_ref[0, r0:r0 + R, :] = x_ref[0, r0:r0 + R, :] + jnp.dot(merged, wo_ref[...],
                                                                    preferred_element_type=jnp.float32)

    n_sub = T // R
    pending = branch_in(0)
    window_sums()
    for j in range(n_sub):
        nxt = branch_in((j + 1) * R) if j + 1 < n_sub else None
        branch_out(j * R, *pending)
        pending = nxt
    uext_ref[0:HALO, :] = uext_ref[T:T + HALO, :]


def _merge(x, a, u, g, w_in, wa, wp, ps, wb, wo, layer):
    B, S, D = x.shape
    A = a.shape[2]
    P = u.shape[2]
    assert 3 * A + P == 2 * D, "the branch gates are addressed as column block 1 of width 2D"
    T = min(ROW_TILE, S)
    assert S % T == 0 and T % SUB_TILE == 0
    row = lambda b, s: (b, s, 0)
    return pl.pallas_call(
        _merge_kernel,
        grid=(B, S // T),
        in_specs=[
            pl.BlockSpec((1, T, D), row),
            pl.BlockSpec((1, T, A), row),
            pl.BlockSpec((1, T, P), row),
            _layer_operand((1, D), layer),
            _layer_operand((D, 2 * D), layer, (0, 1)),
            _layer_operand(wa.shape[1:], layer),
            _layer_operand(wp.shape[1:], layer),
            _layer_operand((1, P), layer),
            _layer_operand(wb.shape[1:], layer),
            _layer_operand(wo.shape[1:], layer),
        ],
        out_specs=pl.BlockSpec((1, T, D), row),
        out_shape=jax.ShapeDtypeStruct((B, S, D), x.dtype),
        scratch_shapes=[pltpu.VMEM((POOL_HALO + T, P), jnp.float32)] * 4,
        compiler_params=pltpu.CompilerParams(
            dimension_semantics=("parallel", "arbitrary"), vmem_limit_bytes=V7X_VMEM_LIMIT),
        name="merge",
    )(x, a, u, g, w_in, wa, wp, ps, wb, wo)


def _ffn_kernel(x_ref, g_ref, wup_ref, cw_ref, cb_ref, wdn_ref, gf_ref, o_ref,
                ext_ref, carry_ref, act_ref, *, final_norm):
    T = x_ref.shape[1]
    F = wdn_ref.shape[0]
    FC = FF_CHUNK
    H = CONV_HALO
    s_idx = pl.program_id(1)

    @pl.when(s_idx == 0)
    def _():
        carry_ref[...] = jnp.zeros(carry_ref.shape, jnp.float32)

    h = _rms_norm(x_ref[0], g_ref[...]).astype(jnp.bfloat16)

    def conv_cols(c0, ext, scale):
        a = jnp.dot(h, wup_ref[:, c0:c0 + FC], preferred_element_type=jnp.float32)
        ext[0:H, :] = carry_ref[:, c0:c0 + FC]
        ext[H:H + T, :] = a
        carry_ref[:, c0:c0 + FC] = a[T - H:, :]
        w = cw_ref[:, c0:c0 + FC] * scale
        return (cb_ref[:, c0:c0 + FC] * scale + w[2:3, :] * a
                + w[1:2, :] * ext[H - 1:H - 1 + T, :]
                + w[0:1, :] * ext[H - 2:H - 2 + T, :])

    for c in range(F // FC):
        half_gate = conv_cols(c * FC, ext_ref.at[(2 * c) % ext_ref.shape[0]], 0.5)
        val = conv_cols(F + c * FC, ext_ref.at[(2 * c + 1) % ext_ref.shape[0]], 1.0)
        act_ref[:, c * FC:(c + 1) * FC] = (half_gate * (jnp.tanh(half_gate) + 1.0) * val).astype(act_ref.dtype)

    y = x_ref[0] + jnp.dot(act_ref[...], wdn_ref[...], preferred_element_type=jnp.float32)
    if final_norm:
        y = _rms_norm(y, gf_ref[...])
    o_ref[0] = y


def _ffn(x, g, wup, cw, cb, wdn, gf, layer, final_norm):
    B, S, D = x.shape
    F = wdn.shape[1]
    T = min(FFN_ROW_TILE, S)
    assert S % T == 0
    row = lambda b, s: (b, s, 0)
    return pl.pallas_call(
        functools.partial(_ffn_kernel, final_norm=final_norm),
        grid=(B, S // T),
        in_specs=[
            pl.BlockSpec((1, T, D), row),
            _layer_operand((1, D), layer),
            _layer_operand(wup.shape[1:], layer),
            _layer_operand(cw.shape[1:], layer),
            _layer_operand(cb.shape[1:], layer),
            _layer_operand(wdn.shape[1:], layer),
            pl.BlockSpec((1, D), lambda b, s: (0, 0), pipeline_mode=pl.Buffered(1)),
        ],
        out_specs=pl.BlockSpec((1, T, D), row),
        out_shape=jax.ShapeDtypeStruct((B, S, D), x.dtype),
        scratch_shapes=[
            pltpu.VMEM((1, CONV_HALO + T, FF_CHUNK), jnp.float32),
            pltpu.VMEM((CONV_HALO, 2 * F), jnp.float32),
            pltpu.VMEM((T, F), jnp.bfloat16),
        ],
        compiler_params=pltpu.CompilerParams(
            dimension_semantics=("parallel", "arbitrary"), vmem_limit_bytes=V7X_VMEM_LIMIT),
        name="ffn",
    )(x, g, wup, cw, cb, wdn, gf)


def kernel(x, norm_mix_g, w_in, w_pool, pool_scale, w_branch_a, w_branch_b, w_out, norm_ffn_g, w_up, conv_w, conv_b, w_down, norm_final_g):
    depth = w_in.shape[0]
    D = x.shape[-1]
    A = w_branch_a.shape[1]
    P = w_branch_b.shape[1]
    assert x.shape[1] % MOBA_BLOCK == 0 and A % (2 * HEAD_DIM) == 0
    assert w_down.shape[1] % FF_CHUNK == 0 and 2 * POOL_WINDOWS[-1] <= POOL_HALO and POOL_WINDOWS == (2, 4, 8, 16) and CONV_WIDTH - 1 <= CONV_HALO
    bf = jnp.bfloat16
    w_in_bf = w_in.astype(bf)
    wvt = jnp.swapaxes(w_in[:, :, 2 * A:3 * A], 1, 2).astype(bf)
    wa, wp, wb, wo = (w.astype(bf) for w in (w_branch_a, w_pool, w_branch_b, w_out))
    wup, wdn = w_up.astype(bf), w_down.astype(bf)
    g_mix, g_ffn = norm_mix_g.reshape(depth, 1, D), norm_ffn_g.reshape(depth, 1, D)
    ps, cb = pool_scale.reshape(depth, 1, P), conv_b.reshape(depth, 1, -1)
    gf = norm_final_g.reshape(1, D)
    for layer in range(depth):
        q, k, vt, u = _in_proj(x, g_mix, w_in_bf, wvt, layer, A, P)
        a = _moba(q, k, vt)
        x = _merge(x, a, u, g_mix, w_in_bf, wa, wp, ps, wb, wo, layer)
        x = _ffn(x, g_ffn, wup, conv_w, cb, wdn, gf, layer, layer == depth - 1)
    return x
```

```python
import functools

import jax
import jax.numpy as jnp
from jax import lax
from jax.experimental import pallas as pl
from jax.experimental.pallas import tpu as pltpu

HEAD_DIM = 64
MOBA_BLOCK = 256
MOBA_TOPK = 3
POOL_WINDOWS = (2, 4, 8, 16)
CONV_WIDTH = 3
RMS_EPS = 1e-6
NEG_INF = -1e30
LOG2_E = 1.4426950408889634

ROW_TILE = 1024
FFN_ROW_TILE = 512
SUB_TILE = 256
POOL_HALO = 32
ONES_ROWS = 16
CONV_HALO = 8
FF_CHUNK = 256
V7X_VMEM_LIMIT = 56 * 1024 * 1024

_NT = (((1,), (1,)), ((), ()))


def _layer_operand(block, layer, index=None):
    index = (0,) * len(block) if index is None else index
    return pl.BlockSpec((None,) + tuple(block), lambda b, s: (layer,) + tuple(index),
                        pipeline_mode=pl.Buffered(1))


def _rms_norm(x, g):
    y = x * lax.rsqrt(jnp.mean(x * x, axis=-1, keepdims=True) + RMS_EPS)
    return y * g


def _in_proj_kernel(x_ref, g_ref, wq_ref, wk_ref, wvt_ref, wu_ref, q_ref, k_ref, vt_ref, u_ref, *, scale):
    for j in range(vt_ref.shape[1]):
        rows = slice(j * MOBA_BLOCK, (j + 1) * MOBA_BLOCK)
        h = _rms_norm(x_ref[0, rows, :], g_ref[...]).astype(jnp.bfloat16)
        q_ref[0, rows, :] = (jnp.dot(h, wq_ref[...], preferred_element_type=jnp.float32) * scale).astype(q_ref.dtype)
        k_ref[0, rows, :] = jnp.dot(h, wk_ref[...], preferred_element_type=jnp.float32).astype(k_ref.dtype)
        u_ref[0, rows, :] = jnp.dot(h, wu_ref[...], preferred_element_type=jnp.float32).astype(u_ref.dtype)
        vt_ref[0, j] = lax.dot_general(wvt_ref[...], h, _NT,
                                       preferred_element_type=jnp.float32).astype(vt_ref.dtype)


def _in_proj(x, g, w_in, wvt, layer, A, P):
    B, S, D = x.shape
    assert P == A, "u is addressed as column block 3 of width A"
    T = min(ROW_TILE, S)
    assert S % T == 0 and T % SUB_TILE == 0
    nb_t = T // MOBA_BLOCK
    nb = S // MOBA_BLOCK
    return pl.pallas_call(
        functools.partial(_in_proj_kernel, scale=HEAD_DIM ** -0.5 * LOG2_E),
        grid=(B, S // T),
        in_specs=[
            pl.BlockSpec((1, T, D), lambda b, s: (b, s, 0)),
            _layer_operand((1, D), layer),
            _layer_operand((D, A), layer, (0, 0)),
            _layer_operand((D, A), layer, (0, 1)),
            _layer_operand((A, D), layer),
            _layer_operand((D, P), layer, (0, 3)),
        ],
        out_specs=[
            pl.BlockSpec((1, T, A), lambda b, s: (b, s, 0)),
            pl.BlockSpec((1, T, A), lambda b, s: (b, s, 0)),
            pl.BlockSpec((1, nb_t, A, MOBA_BLOCK), lambda b, s: (b, s, 0, 0)),
            pl.BlockSpec((1, T, P), lambda b, s: (b, s, 0)),
        ],
        out_shape=[
            jax.ShapeDtypeStruct((B, S, A), jnp.bfloat16),
            jax.ShapeDtypeStruct((B, S, A), jnp.bfloat16),
            jax.ShapeDtypeStruct((B, nb, A, MOBA_BLOCK), jnp.bfloat16),
            jax.ShapeDtypeStruct((B, S, P), jnp.bfloat16),
        ],
        compiler_params=pltpu.CompilerParams(
            dimension_semantics=("parallel", "parallel"), vmem_limit_bytes=V7X_VMEM_LIMIT),
        name="in_proj",
    )(x, g, w_in, w_in, wvt, w_in)


def _moba_kernel(q_ref, k_ref, vt_ref, o_ref, kmean_ref, qz_ref, s0_ref, mb0_ref, s1_ref, mb1_ref, ot_ref,
                 *, n_heads, n_blocks):
    L = MOBA_BLOCK
    G = 2 * HEAD_DIM
    i = pl.program_id(1)

    @pl.when(i == 0)
    def _():
        row = lax.broadcasted_iota(jnp.int32, (n_blocks, n_blocks * L), 0)
        col = lax.broadcasted_iota(jnp.int32, (n_blocks, n_blocks * L), 1)
        ind = jnp.where((col >= row * L) & (col < (row + 1) * L), 1.0 / L, 0.0).astype(jnp.bfloat16)
        k_all = k_ref[0].reshape(n_blocks * L, k_ref.shape[-1])
        kmean_ref[...] = jnp.dot(ind, k_all, preferred_element_type=jnp.float32)

    blk = lax.broadcasted_iota(jnp.int32, (n_blocks, L), 0)
    key_pos = lax.broadcasted_iota(jnp.int32, (L, L), 0)
    qry_pos = lax.broadcasted_iota(jnp.int32, (L, L), 1)
    lane = lax.broadcasted_iota(jnp.int32, (L, G), 1)
    ones_rows = jnp.ones((ONES_ROWS, L), jnp.bfloat16)
    groups = [slice((h // 2) * G, (h // 2 + 1) * G) for h in range(n_heads)]

    def scores(b, h):
        return lax.dot_general(k_ref[0, b, :, groups[h]], qz_ref[h], _NT, preferred_element_type=jnp.float32)

    def pv(b, h, p):
        lhs = jnp.concatenate([vt_ref[0, b, h * HEAD_DIM:(h + 1) * HEAD_DIM, :], ones_rows], axis=0)
        return jnp.dot(lhs, p.astype(jnp.bfloat16), preferred_element_type=jnp.float32)

    def stage_a(b, h, slot, causal):
        s_ref, mb_ref = slot
        s = scores(b, h)
        if causal:
            s = jnp.where(key_pos <= qry_pos, s, NEG_INF)
        s_ref[h] = s
        mb_ref[h:h + 1, :] = jnp.max(s, axis=0, keepdims=True)

    def stage_b(b, h, slot, state, on):
        s_ref, mb_ref = slot
        mb = mb_ref[h:h + 1, :]
        if state is None:
            return mb, pv(b, h, jnp.exp2(s_ref[h] - mb))
        m, acc = state
        if on is None:
            m_new = jnp.maximum(m, mb)
            shift = m_new
        else:
            m_new = jnp.where(on, jnp.maximum(m, mb), m)
            shift = jnp.where(on, m_new, -NEG_INF)
        return m_new, jnp.exp2(m - m_new) * acc + pv(b, h, jnp.exp2(s_ref[h] - shift))

    def query_tile(ti):
        sel = []
        for h in range(n_heads):
            in_head = (lane >= (h % 2) * HEAD_DIM) & (lane < (h % 2 + 1) * HEAD_DIM)
            qz_ref[h] = jnp.where(in_head, q_ref[0, :, groups[h]], jnp.zeros((L, G), q_ref.dtype))
            if ti <= MOBA_TOPK:
                sel.append(None)
                continue
            km = kmean_ref[:, groups[h]]
            km_hi = km.astype(jnp.bfloat16)
            km_lo = (km - km_hi.astype(jnp.float32)).astype(jnp.bfloat16)
            g2 = lax.dot_general(jnp.concatenate([km_hi, km_lo], axis=0), qz_ref[h], _NT,
                                 preferred_element_type=jnp.float32)
            gate = g2[:n_blocks] + g2[n_blocks:]
            rank = jnp.zeros((n_blocks, L), jnp.float32)
            for j in range(ti):
                gj = gate[j:j + 1, :]
                rank = rank + jnp.where((gj > gate) | ((gj == gate) & (blk > j)), 1.0, 0.0)
            sel.append(jnp.where(rank < MOBA_TOPK, 1.0, 0.0))

        slots = ((s0_ref, mb0_ref), (s1_ref, mb1_ref))
        order = [ti] + list(range(ti))
        for h in range(n_heads):
            stage_a(ti, h, slots[0], causal=True)
        state = [None] * n_heads
        for t, b in enumerate(order):
            for h in range(n_heads):
                if t + 1 < len(order):
                    stage_a(order[t + 1], h, slots[(t + 1) % 2], causal=False)
                on = None if (t == 0 or sel[h] is None) else sel[h][b:b + 1, :] > 0.0
                state[h] = stage_b(b, h, slots[t % 2], state[h], on)

        for h in range(n_heads):
            acc = state[h][1]
            inv_l = 1.0 / acc[HEAD_DIM:HEAD_DIM + 1, :]
            ot_ref[h * HEAD_DIM:(h + 1) * HEAD_DIM, :] = (acc[0:HEAD_DIM, :] * inv_l).astype(ot_ref.dtype)
        eye = jnp.where(key_pos == qry_pos, 1.0, 0.0).astype(jnp.bfloat16)
        o_ref[0] = lax.dot_general(eye, ot_ref[...], _NT, preferred_element_type=jnp.float32).astype(o_ref.dtype)

    for ti in range(n_blocks):
        pl.when(i == ti)(functools.partial(query_tile, ti))


def _moba(q, k, vt):
    B, S, A = q.shape
    L = MOBA_BLOCK
    nb = S // L
    H = A // HEAD_DIM
    k4 = k.reshape(B, nb, L, A)
    return pl.pallas_call(
        functools.partial(_moba_kernel, n_heads=H, n_blocks=nb),
        grid=(B, nb),
        in_specs=[
            pl.BlockSpec((1, L, A), lambda b, i: (b, i, 0)),
            pl.BlockSpec((1, nb, L, A), lambda b, i: (b, 0, 0, 0)),
            pl.BlockSpec((1, nb, A, L), lambda b, i: (b, 0, 0, 0)),
        ],
        out_specs=pl.BlockSpec((1, L, A), lambda b, i: (b, i, 0)),
        out_shape=jax.ShapeDtypeStruct((B, S, A), jnp.bfloat16),
        scratch_shapes=[
            pltpu.VMEM((nb, A), jnp.float32),
            pltpu.VMEM((H, L, 2 * HEAD_DIM), jnp.bfloat16),
            pltpu.VMEM((H, L, L), jnp.float32),
            pltpu.VMEM((H, L), jnp.float32),
            pltpu.VMEM((H, L, L), jnp.float32),
            pltpu.VMEM((H, L), jnp.float32),
            pltpu.VMEM((A, L), jnp.bfloat16),
        ],
        compiler_params=pltpu.CompilerParams(
            dimension_semantics=("parallel", "arbitrary"), vmem_limit_bytes=V7X_VMEM_LIMIT),
        name="moba",
    )(q, k4, vt)


def _sigmoid(x):
    return 0.5 * jnp.tanh(0.5 * x) + 0.5


def _merge_kernel(x_ref, a_ref, u_ref, g_ref, wg_ref, wa_ref, wp_ref, ps_ref, wb_ref, wo_ref,
                  o_ref, uext_ref, w2_ref, w4_ref, w8_ref):
    T = x_ref.shape[1]
    D = x_ref.shape[2]
    GW = wp_ref.shape[1]
    R = SUB_TILE
    HALO = POOL_HALO
    s_idx = pl.program_id(1)

    @pl.when(s_idx == 0)
    def _():
        uext_ref[0:HALO, :] = jnp.zeros((HALO, uext_ref.shape[1]), jnp.float32)

    def branch_in(r0):
        y_a = jnp.dot(a_ref[0, r0:r0 + R, :], wa_ref[...], preferred_element_type=jnp.float32)
        h = _rms_norm(x_ref[0, r0:r0 + R, :], g_ref[...]).astype(jnp.bfloat16)
        return y_a, jnp.dot(h, wg_ref[...], preferred_element_type=jnp.float32)

    def window_sums():
        n = HALO + T
        uext_ref[HALO:n, :] = u_ref[0].astype(jnp.float32)
        w2_ref[8:n, :] = uext_ref[8:n, :] + uext_ref[7:n - 1, :]
        w4_ref[16:n, GW:] = w2_ref[16:n, GW:] + w2_ref[14:n - 2, GW:]
        w8_ref[24:n, 2 * GW:] = w4_ref[24:n, 2 * GW:] + w4_ref[20:n - 4, 2 * GW:]

    def branch_out(r0, y_a, g_pre):
        e0 = HALO + r0
        wsums = (w2_ref[e0:e0 + R, 0:GW], w4_ref[e0:e0 + R, GW:2 * GW], w8_ref[e0:e0 + R, 2 * GW:3 * GW],
                 w8_ref[e0:e0 + R, 3 * GW:] + w8_ref[e0 - 8:e0 - 8 + R, 3 * GW:])
        t_pos = s_idx * T + r0 + lax.broadcasted_iota(jnp.int32, (R, GW), 0)
        ys = []
        for gi, win in enumerate(POOL_WINDOWS):
            count = jnp.minimum(t_pos + 1, win).astype(jnp.float32)
            mixed = (wsums[gi] / count - uext_ref[e0:e0 + R, gi * GW:(gi + 1) * GW]).astype(jnp.bfloat16)
            ys.append(jnp.dot(mixed, wp_ref[gi], preferred_element_type=jnp.float32))
        pooled = (jnp.concatenate(ys, axis=1) * ps_ref[...]).astype(jnp.bfloat16)
        y_b = jnp.dot(pooled, wb_ref[...], preferred_element_type=jnp.float32)
        gates = _sigmoid(g_pre)
        merged = (gates[:, :D] * y_a + gates[:, D:] * y_b).astype(jnp.bfloat16)
        o_ref[0, r0:r0 + R, :] = x_ref[0, r0:r0 + R, :] + jnp.dot(merged, wo_ref[...],
                                                                    preferred_element_type=jnp.float32)

    n_sub = T // R
    pending = branch_in(0)
    window_sums()
    for j in range(n_sub):
        nxt = branch_in((j + 1) * R) if j + 1 < n_sub else None
        branch_out(j * R, *pending)
        pending = nxt
    uext_ref[0:HALO, :] = uext_ref[T:T + HALO, :]


def _merge(x, a, u, g, w_in, wa, wp, ps, wb, wo, layer):
    B, S, D = x.shape
    A = a.shape[2]
    P = u.shape[2]
    assert 3 * A + P == 2 * D, "the branch gates are addressed as column block 1 of width 2D"
    T = min(ROW_TILE, S)
    assert S % T == 0 and T % SUB_TILE == 0
    row = lambda b, s: (b, s, 0)
    return pl.pallas_call(
        _merge_kernel,
        grid=(B, S // T),
        in_specs=[
            pl.BlockSpec((1, T, D), row),
            pl.BlockSpec((1, T, A), row),
            pl.BlockSpec((1, T, P), row),
            _layer_operand((1, D), layer),
            _layer_operand((D, 2 * D), layer, (0, 1)),
            _layer_operand(wa.shape[1:], layer),
            _layer_operand(wp.shape[1:], layer),
            _layer_operand((1, P), layer),
            _layer_operand(wb.shape[1:], layer),
            _layer_operand(wo.shape[1:], layer),
        ],
        out_specs=pl.BlockSpec((1, T, D), row),
        out_shape=jax.ShapeDtypeStruct((B, S, D), x.dtype),
        scratch_shapes=[pltpu.VMEM((POOL_HALO + T, P), jnp.float32)] * 4,
        compiler_params=pltpu.CompilerParams(
            dimension_semantics=("parallel", "arbitrary"), vmem_limit_bytes=V7X_VMEM_LIMIT),
        name="merge",
    )(x, a, u, g, w_in, wa, wp, ps, wb, wo)


def _ffn_kernel(x_ref, g_ref, wup_ref, cw_ref, cb_ref, wdn_ref, gf_ref, o_ref,
                ext_ref, carry_ref, act_ref, *, final_norm):
    T = x_ref.shape[1]
    F = wdn_ref.shape[0]
    FC = FF_CHUNK
    H = CONV_HALO
    s_idx = pl.program_id(1)

    @pl.when(s_idx == 0)
    def _():
        carry_ref[...] = jnp.zeros(carry_ref.shape, jnp.float32)

    h = _rms_norm(x_ref[0], g_ref[...]).astype(jnp.bfloat16)

    def conv_cols(c0, ext, scale):
        a = jnp.dot(h, wup_ref[:, c0:c0 + FC], preferred_element_type=jnp.float32)
        ext[0:H, :] = carry_ref[:, c0:c0 + FC]
        ext[H:H + T, :] = a
        carry_ref[:, c0:c0 + FC] = a[T - H:, :]
        w = cw_ref[:, c0:c0 + FC] * scale
        return (cb_ref[:, c0:c0 + FC] * scale + w[2:3, :] * a
                + w[1:2, :] * ext[H - 1:H - 1 + T, :]
                + w[0:1, :] * ext[H - 2:H - 2 + T, :])

    for c in range(F // FC):
        half_gate = conv_cols(c * FC, ext_ref.at[(2 * c) % ext_ref.shape[0]], 0.5)
        val = conv_cols(F + c * FC, ext_ref.at[(2 * c + 1) % ext_ref.shape[0]], 1.0)
        act_ref[:, c * FC:(c + 1) * FC] = (half_gate * (jnp.tanh(half_gate) + 1.0) * val).astype(act_ref.dtype)

    y = x_ref[0] + jnp.dot(act_ref[...], wdn_ref[...], preferred_element_type=jnp.float32)
    if final_norm:
        y = _rms_norm(y, gf_ref[...])
    o_ref[0] = y


def _ffn(x, g, wup, cw, cb, wdn, gf, layer, final_norm):
    B, S, D = x.shape
    F = wdn.shape[1]
    T = min(FFN_ROW_TILE, S)
    assert S % T == 0
    row = lambda b, s: (b, s, 0)
    return pl.pallas_call(
        functools.partial(_ffn_kernel, final_norm=final_norm),
        grid=(B, S // T),
        in_specs=[
            pl.BlockSpec((1, T, D), row),
            _layer_operand((1, D), layer),
            _layer_operand(wup.shape[1:], layer),
            _layer_operand(cw.shape[1:], layer),
            _layer_operand(cb.shape[1:], layer),
            _layer_operand(wdn.shape[1:], layer),
            pl.BlockSpec((1, D), lambda b, s: (0, 0), pipeline_mode=pl.Buffered(1)),
        ],
        out_specs=pl.BlockSpec((1, T, D), row),
        out_shape=jax.ShapeDtypeStruct((B, S, D), x.dtype),
        scratch_shapes=[
            pltpu.VMEM((1, CONV_HALO + T, FF_CHUNK), jnp.float32),
            pltpu.VMEM((CONV_HALO, 2 * F), jnp.float32),
            pltpu.VMEM((T, F), jnp.bfloat16),
        ],
        compiler_params=pltpu.CompilerParams(
            dimension_semantics=("parallel", "arbitrary"), vmem_limit_bytes=V7X_VMEM_LIMIT),
        name="ffn",
    )(x, g, wup, cw, cb, wdn, gf)


def kernel(x, norm_mix_g, w_in, w_pool, pool_scale, w_branch_a, w_branch_b, w_out, norm_ffn_g, w_up, conv_w, conv_b, w_down, norm_final_g):
    depth = w_in.shape[0]
    D = x.shape[-1]
    A = w_branch_a.shape[1]
    P = w_branch_b.shape[1]
    assert x.shape[1] % MOBA_BLOCK == 0 and A % (2 * HEAD_DIM) == 0
    assert w_down.shape[1] % FF_CHUNK == 0 and 2 * POOL_WINDOWS[-1] <= POOL_HALO and POOL_WINDOWS == (2, 4, 8, 16) and CONV_WIDTH - 1 <= CONV_HALO
    bf = jnp.bfloat16
    w_in_bf = w_in.astype(bf)
    wvt = jnp.swapaxes(w_in[:, :, 2 * A:3 * A], 1, 2).astype(bf)
    wa, wp, wb, wo = (w.astype(bf) for w in (w_branch_a, w_pool, w_branch_b, w_out))
    wup, wdn = w_up.astype(bf), w_down.astype(bf)
    g_mix, g_ffn = norm_mix_g.reshape(depth, 1, D), norm_ffn_g.reshape(depth, 1, D)
    ps, cb = pool_scale.reshape(depth, 1, P), conv_b.reshape(depth, 1, -1)
    gf = norm_final_g.reshape(1, D)
    for layer in range(depth):
        q, k, vt, u = _in_proj(x, g_mix, w_in_bf, wvt, layer, A, P)
        a = _moba(q, k, vt)
        x = _merge(x, a, u, g_mix, w_in_bf, wa, wp, ps, wb, wo, layer)
        x = _ffn(x, g_ffn, wup, conv_w, cb, wdn, gf, layer, layer == depth - 1)
    return x
```

```python
import functools

import jax
import jax.numpy as jnp
from jax import lax
from jax.experimental import pallas as pl
from jax.experimental.pallas import tpu as pltpu

HEAD_DIM = 64
MOBA_BLOCK = 256
MOBA_TOPK = 3
POOL_WINDOWS = (2, 4, 8, 16)
CONV_WIDTH = 3
RMS_EPS = 1e-6
NEG_INF = -1e30
LOG2_E = 1.4426950408889634

ROW_TILE = 1024
FFN_ROW_TILE = 512
SUB_TILE = 256
POOL_HALO = 32
ONES_ROWS = 16
CONV_HALO = 8
FF_CHUNK = 256
V7X_VMEM_LIMIT = 56 * 1024 * 1024

_NT = (((1,), (1,)), ((), ()))


def _layer_operand(block, layer, index=None):
    index = (0,) * len(block) if index is None else index
    return pl.BlockSpec((None,) + tuple(block), lambda b, s: (layer,) + tuple(index),
                        pipeline_mode=pl.Buffered(1))


def _rms_norm(x, g):
    y = x * lax.rsqrt(jnp.mean(x * x, axis=-1, keepdims=True) + RMS_EPS)
    return y * g


def _in_proj_kernel(x_ref, g_ref, wq_ref, wk_ref, wvt_ref, wu_ref, q_ref, k_ref, vt_ref, u_ref, *, scale):
    for j in range(vt_ref.shape[1]):
        rows = slice(j * MOBA_BLOCK, (j + 1) * MOBA_BLOCK)
        h = _rms_norm(x_ref[0, rows, :], g_ref[...]).astype(jnp.bfloat16)
        q_ref[0, rows, :] = (jnp.dot(h, wq_ref[...], preferred_element_type=jnp.float32) * scale).astype(q_ref.dtype)
        k_ref[0, rows, :] = jnp.dot(h, wk_ref[...], preferred_element_type=jnp.float32).astype(k_ref.dtype)
        u_ref[0, rows, :] = jnp.dot(h, wu_ref[...], preferred_element_type=jnp.float32).astype(u_ref.dtype)
        vt_ref[0, j] = lax.dot_general(wvt_ref[...], h, _NT,
                                       preferred_element_type=jnp.float32).astype(vt_ref.dtype)


def _in_proj(x, g, w_in, wvt, layer, A, P):
    B, S, D = x.shape
    assert P == A, "u is addressed as column block 3 of width A"
    T = min(ROW_TILE, S)
    assert S % T == 0 and T % SUB_TILE == 0
    nb_t = T // MOBA_BLOCK
    nb = S // MOBA_BLOCK
    return pl.pallas_call(
        functools.partial(_in_proj_kernel, scale=HEAD_DIM ** -0.5 * LOG2_E),
        grid=(B, S // T),
        in_specs=[
            pl.BlockSpec((1, T, D), lambda b, s: (b, s, 0)),
            _layer_operand((1, D), layer),
            _layer_operand((D, A), layer, (0, 0)),
            _layer_operand((D, A), layer, (0, 1)),
            _layer_operand((A, D), layer),
            _layer_operand((D, P), layer, (0, 3)),
        ],
        out_specs=[
            pl.BlockSpec((1, T, A), lambda b, s: (b, s, 0)),
            pl.BlockSpec((1, T, A), lambda b, s: (b, s, 0)),
            pl.BlockSpec((1, nb_t, A, MOBA_BLOCK), lambda b, s: (b, s, 0, 0)),
            pl.BlockSpec((1, T, P), lambda b, s: (b, s, 0)),
        ],
        out_shape=[
            jax.ShapeDtypeStruct((B, S, A), jnp.bfloat16),
            jax.ShapeDtypeStruct((B, S, A), jnp.bfloat16),
            jax.ShapeDtypeStruct((B, nb, A, MOBA_BLOCK), jnp.bfloat16),
            jax.ShapeDtypeStruct((B, S, P), jnp.bfloat16),
        ],
        compiler_params=pltpu.CompilerParams(
            dimension_semantics=("parallel", "parallel"), vmem_limit_bytes=V7X_VMEM_LIMIT),
        name="in_proj",
    )(x, g, w_in, w_in, wvt, w_in)


def _moba_kernel(qa_ref, qb_ref, k_ref, vt_ref, oa_ref, ob_ref, kmean_ref, qza_ref, qzb_ref,
                 s0_ref, mb0_ref, s1_ref, mb1_ref, ota_ref, otb_ref, *, n_heads, n_blocks):
    L = MOBA_BLOCK
    G = 2 * HEAD_DIM
    pair = pl.program_id(1)

    @pl.when(pair == 0)
    def _():
        row = lax.broadcasted_iota(jnp.int32, (n_blocks, n_blocks * L), 0)
        col = lax.broadcasted_iota(jnp.int32, (n_blocks, n_blocks * L), 1)
        ind = jnp.where((col >= row * L) & (col < (row + 1) * L), 1.0 / L, 0.0).astype(jnp.bfloat16)
        k_all = k_ref[0].reshape(n_blocks * L, k_ref.shape[-1])
        kmean_ref[...] = jnp.dot(ind, k_all, preferred_element_type=jnp.float32)

    blk = lax.broadcasted_iota(jnp.int32, (n_blocks, L), 0)
    key_pos = lax.broadcasted_iota(jnp.int32, (L, L), 0)
    qry_pos = lax.broadcasted_iota(jnp.int32, (L, L), 1)
    lane = lax.broadcasted_iota(jnp.int32, (L, G), 1)
    ones_rows = jnp.ones((ONES_ROWS, L), jnp.bfloat16)
    groups = [slice((h // 2) * G, (h // 2 + 1) * G) for h in range(n_heads)]
    slots = ((s0_ref, mb0_ref), (s1_ref, mb1_ref))

    def pv(b, h, p):
        lhs = jnp.concatenate([vt_ref[0, b, h * HEAD_DIM:(h + 1) * HEAD_DIM, :], ones_rows], axis=0)
        return jnp.dot(lhs, p.astype(jnp.bfloat16), preferred_element_type=jnp.float32)

    def stage_a(qz_ref, b, h, slot, causal):
        s_ref, mb_ref = slot
        s = lax.dot_general(k_ref[0, b, :, groups[h]], qz_ref[h], _NT, preferred_element_type=jnp.float32)
        if causal:
            s = jnp.where(key_pos <= qry_pos, s, NEG_INF)
        s_ref[h] = s
        mb_ref[h:h + 1, :] = jnp.max(s, axis=0, keepdims=True)

    def stage_b(b, h, slot, state, on):
        s_ref, mb_ref = slot
        mb = mb_ref[h:h + 1, :]
        if state is None:
            return mb, pv(b, h, jnp.exp2(s_ref[h] - mb))
        m, acc = state
        if on is None:
            m_new = jnp.maximum(m, mb)
            shift = m_new
        else:
            m_new = jnp.where(on, jnp.maximum(m, mb), m)
            shift = jnp.where(on, m_new, -NEG_INF)
        return m_new, jnp.exp2(m - m_new) * acc + pv(b, h, jnp.exp2(s_ref[h] - shift))

    def prepare(ti, q_ref, qz_ref):
        sel = []
        for h in range(n_heads):
            in_head = (lane >= (h % 2) * HEAD_DIM) & (lane < (h % 2 + 1) * HEAD_DIM)
            qz_ref[h] = jnp.where(in_head, q_ref[0, :, groups[h]], jnp.zeros((L, G), q_ref.dtype))
            if ti <= MOBA_TOPK:
                sel.append(None)
                continue
            km = kmean_ref[:, groups[h]]
            km_hi = km.astype(jnp.bfloat16)
            km_lo = (km - km_hi.astype(jnp.float32)).astype(jnp.bfloat16)
            g2 = lax.dot_general(jnp.concatenate([km_hi, km_lo], axis=0), qz_ref[h], _NT,
                                 preferred_element_type=jnp.float32)
            gate = g2[:n_blocks] + g2[n_blocks:]
            rank = jnp.zeros((n_blocks, L), jnp.float32)
            for j in range(ti):
                gj = gate[j:j + 1, :]
                rank = rank + jnp.where((gj > gate) | ((gj == gate) & (blk > j)), 1.0, 0.0)
            sel.append(jnp.where(rank < MOBA_TOPK, 1.0, 0.0))
        return sel

    def finalize(state, ot_ref, o_ref):
        for h in range(n_heads):
            acc = state[h][1]
            inv_l = 1.0 / acc[HEAD_DIM:HEAD_DIM + 1, :]
            ot_ref[h * HEAD_DIM:(h + 1) * HEAD_DIM, :] = (acc[0:HEAD_DIM, :] * inv_l).astype(ot_ref.dtype)
        eye = jnp.where(key_pos == qry_pos, 1.0, 0.0).astype(jnp.bfloat16)
        o_ref[0] = lax.dot_general(eye, ot_ref[...], _NT, preferred_element_type=jnp.float32).astype(o_ref.dtype)

    def tile_pair(tp):
        tiles = ((tp, qa_ref, qza_ref, ota_ref, oa_ref), (n_blocks - 1 - tp, qb_ref, qzb_ref, otb_ref, ob_ref))
        sels = [prepare(ti, q_ref, qz_ref) for ti, q_ref, qz_ref, _, _ in tiles]
        visits = [(w, b, n == 0, n == tiles[w][0]) for w in range(2)
                  for n, b in enumerate([tiles[w][0]] + list(range(tiles[w][0])))]
        states = [[None] * n_heads, [None] * n_heads]
        for h in range(n_heads):
            stage_a(tiles[0][2], visits[0][1], h, slots[0], causal=True)
        for n, (w, b, first, last) in enumerate(visits):
            for h in range(n_heads):
                if n + 1 < len(visits):
                    w2, b2, first2, _ = visits[n + 1]
                    stage_a(tiles[w2][2], b2, h, slots[(n + 1) % 2], causal=first2)
                on = None if (first or sels[w][h] is None) else sels[w][h][b:b + 1, :] > 0.0
                states[w][h] = stage_b(b, h, slots[n % 2], states[w][h], on)
            if last:
                finalize(states[w], tiles[w][3], tiles[w][4])

    for tp in range(n_blocks // 2):
        pl.when(pair == tp)(functools.partial(tile_pair, tp))


def _moba(q, k, vt):
    B, S, A = q.shape
    L = MOBA_BLOCK
    nb = S // L
    H = A // HEAD_DIM
    assert nb % 2 == 0
    k4 = k.reshape(B, nb, L, A)
    half = jax.ShapeDtypeStruct((B, S // 2, A), jnp.bfloat16)
    return pl.pallas_call(
        functools.partial(_moba_kernel, n_heads=H, n_blocks=nb),
        grid=(B, nb // 2),
        in_specs=[
            pl.BlockSpec((1, L, A), lambda b, j: (b, j, 0)),
            pl.BlockSpec((1, L, A), lambda b, j: (b, nb - 1 - j, 0)),
            pl.BlockSpec((1, nb, L, A), lambda b, j: (b, 0, 0, 0)),
            pl.BlockSpec((1, nb, A, L), lambda b, j: (b, 0, 0, 0)),
        ],
        out_specs=[
            pl.BlockSpec((1, L, A), lambda b, j: (b, j, 0)),
            pl.BlockSpec((1, L, A), lambda b, j: (b, nb // 2 - 1 - j, 0)),
        ],
        out_shape=[half, half],
        scratch_shapes=[
            pltpu.VMEM((nb, A), jnp.float32),
            pltpu.VMEM((H, L, 2 * HEAD_DIM), jnp.bfloat16),
            pltpu.VMEM((H, L, 2 * HEAD_DIM), jnp.bfloat16),
            pltpu.VMEM((H, L, L), jnp.float32),
            pltpu.VMEM((H, L), jnp.float32),
            pltpu.VMEM((H, L, L), jnp.float32),
            pltpu.VMEM((H, L), jnp.float32),
            pltpu.VMEM((A, L), jnp.bfloat16),
            pltpu.VMEM((A, L), jnp.bfloat16),
        ],
        compiler_params=pltpu.CompilerParams(
            dimension_semantics=("parallel", "arbitrary"), vmem_limit_bytes=V7X_VMEM_LIMIT),
        name="moba",
    )(q, q, k4, vt)


def _sigmoid(x):
    return 0.5 * jnp.tanh(0.5 * x) + 0.5


def _merge_kernel(x_ref, a_lo_ref, a_hi_ref, u_ref, g_ref, wg_ref, wa_ref, wp_ref, ps_ref, wb_ref, wo_ref,
                  o_ref, uext_ref, w2_ref, w4_ref, w8_ref, *, n_lo):
    T = x_ref.shape[1]
    D = x_ref.shape[2]
    GW = wp_ref.shape[1]
    R = SUB_TILE
    HALO = POOL_HALO
    s_idx = pl.program_id(1)

    @pl.when(s_idx == 0)
    def _():
        uext_ref[0:HALO, :] = jnp.zeros((HALO, uext_ref.shape[1]), jnp.float32)

    def branch_in(r0):
        a = jnp.where(s_idx < n_lo, a_lo_ref[0, r0:r0 + R, :], a_hi_ref[0, r0:r0 + R, :])
        y_a = jnp.dot(a, wa_ref[...], preferred_element_type=jnp.float32)
        h = _rms_norm(x_ref[0, r0:r0 + R, :], g_ref[...]).astype(jnp.bfloat16)
        return y_a, jnp.dot(h, wg_ref[...], preferred_element_type=jnp.float32)

    def window_sums():
        n = HALO + T
        uext_ref[HALO:n, :] = u_ref[0].astype(jnp.float32)
        w2_ref[8:n, :] = uext_ref[8:n, :] + uext_ref[7:n - 1, :]
        w4_ref[16:n, GW:] = w2_ref[16:n, GW:] + w2_ref[14:n - 2, GW:]
        w8_ref[24:n, 2 * GW:] = w4_ref[24:n, 2 * GW:] + w4_ref[20:n - 4, 2 * GW:]

    def branch_out(r0, y_a, g_pre):
        e0 = HALO + r0
        wsums = (w2_ref[e0:e0 + R, 0:GW], w4_ref[e0:e0 + R, GW:2 * GW], w8_ref[e0:e0 + R, 2 * GW:3 * GW],
                 w8_ref[e0:e0 + R, 3 * GW:] + w8_ref[e0 - 8:e0 - 8 + R, 3 * GW:])
        t_pos = s_idx * T + r0 + lax.broadcasted_iota(jnp.int32, (R, GW), 0)
        ys = []
        for gi, win in enumerate(POOL_WINDOWS):
            count = jnp.minimum(t_pos + 1, win).astype(jnp.float32)
            mixed = (wsums[gi] / count - uext_ref[e0:e0 + R, gi * GW:(gi + 1) * GW]).astype(jnp.bfloat16)
            ys.append(jnp.dot(mixed, wp_ref[gi], preferred_element_type=jnp.float32))
        pooled = (jnp.concatenate(ys, axis=1) * ps_ref[...]).astype(jnp.bfloat16)
        y_b = jnp.dot(pooled, wb_ref[...], preferred_element_type=jnp.float32)
        gates = _sigmoid(g_pre)
        merged = (gates[:, :D] * y_a + gates[:, D:] * y_b).astype(jnp.bfloat16)
        o_ref[0, r0:r0 + R, :] = x_ref[0, r0:r0 + R, :] + jnp.dot(merged, wo_ref[...],
                                                                    preferred_element_type=jnp.float32)

    n_sub = T // R
    pending = branch_in(0)
    window_sums()
    for j in range(n_sub):
        nxt = branch_in((j + 1) * R) if j + 1 < n_sub else None
        branch_out(j * R, *pending)
        pending = nxt
    uext_ref[0:HALO, :] = uext_ref[T:T + HALO, :]


def _merge(x, a_lo, a_hi, u, g, w_in, wa, wp, ps, wb, wo, layer):
    B, S, D = x.shape
    A = a_lo.shape[2]
    P = u.shape[2]
    assert 3 * A + P == 2 * D, "the branch gates are addressed as column block 1 of width 2D"
    T = min(ROW_TILE, S)
    assert S % T == 0 and T % SUB_TILE == 0
    row = lambda b, s: (b, s, 0)
    n_lo = S // T // 2
    assert S % (2 * T) == 0
    return pl.pallas_call(
        functools.partial(_merge_kernel, n_lo=n_lo),
        grid=(B, S // T),
        in_specs=[
            pl.BlockSpec((1, T, D), row),
            pl.BlockSpec((1, T, A), lambda b, s: (b, jnp.minimum(s, n_lo - 1), 0)),
            pl.BlockSpec((1, T, A), lambda b, s: (b, jnp.maximum(s - n_lo, 0), 0)),
            pl.BlockSpec((1, T, P), row),
            _layer_operand((1, D), layer),
            _layer_operand((D, 2 * D), layer, (0, 1)),
            _layer_operand(wa.shape[1:], layer),
            _layer_operand(wp.shape[1:], layer),
            _layer_operand((1, P), layer),
            _layer_operand(wb.shape[1:], layer),
            _layer_operand(wo.shape[1:], layer),
        ],
        out_specs=pl.BlockSpec((1, T, D), row),
        out_shape=jax.ShapeDtypeStruct((B, S, D), x.dtype),
        scratch_shapes=[pltpu.VMEM((POOL_HALO + T, P), jnp.float32)] * 4,
        compiler_params=pltpu.CompilerParams(
            dimension_semantics=("parallel", "arbitrary"), vmem_limit_bytes=V7X_VMEM_LIMIT),
        name="merge",
    )(x, a_lo, a_hi, u, g, w_in, wa, wp, ps, wb, wo)


def _ffn_kernel(x_ref, g_ref, wup_ref, cw_ref, cb_ref, wdn_ref, gf_ref, o_ref,
                ext_ref, carry_ref, act_ref, *, final_norm):
    T = x_ref.shape[1]
    F = wdn_ref.shape[0]
    FC = FF_CHUNK
    H = CONV_HALO
    s_idx = pl.program_id(1)

    @pl.when(s_idx == 0)
    def _():
        carry_ref[...] = jnp.zeros(carry_ref.shape, jnp.float32)

    h = _rms_norm(x_ref[0], g_ref[...]).astype(jnp.bfloat16)

    def conv_cols(c0, scale):
        a = jnp.dot(h, wup_ref[:, c0:c0 + FC], preferred_element_type=jnp.float32)
        ext_ref[0:H, :] = carry_ref[:, c0:c0 + FC]
        ext_ref[H:H + T, :] = a
        carry_ref[:, c0:c0 + FC] = a[T - H:, :]
        w = cw_ref[:, c0:c0 + FC] * scale
        return (cb_ref[:, c0:c0 + FC] * scale + w[2:3, :] * a
                + w[1:2, :] * ext_ref[H - 1:H - 1 + T, :]
                + w[0:1, :] * ext_ref[H - 2:H - 2 + T, :])

    for c in range(F // FC):
        half_gate = conv_cols(c * FC, 0.5)
        val = conv_cols(F + c * FC, 1.0)
        act_ref[:, c * FC:(c + 1) * FC] = (half_gate * (jnp.tanh(half_gate) + 1.0) * val).astype(act_ref.dtype)

    y = x_ref[0] + jnp.dot(act_ref[...], wdn_ref[...], preferred_element_type=jnp.float32)
    if final_norm:
        y = _rms_norm(y, gf_ref[...])
    o_ref[0] = y


def _ffn(x, g, wup, cw, cb, wdn, gf, layer, final_norm):
    B, S, D = x.shape
    F = wdn.shape[1]
    T = min(FFN_ROW_TILE, S)
    assert S % T == 0
    row = lambda b, s: (b, s, 0)
    return pl.pallas_call(
        functools.partial(_ffn_kernel, final_norm=final_norm),
        grid=(B, S // T),
        in_specs=[
            pl.BlockSpec((1, T, D), row),
            _layer_operand((1, D), layer),
            _layer_operand(wup.shape[1:], layer),
            _layer_operand(cw.shape[1:], layer),
            _layer_operand(cb.shape[1:], layer),
            _layer_operand(wdn.shape[1:], layer),
            pl.BlockSpec((1, D), lambda b, s: (0, 0), pipeline_mode=pl.Buffered(1)),
        ],
        out_specs=pl.BlockSpec((1, T, D), row),
        out_shape=jax.ShapeDtypeStruct((B, S, D), x.dtype),
        scratch_shapes=[
            pltpu.VMEM((CONV_HALO + T, FF_CHUNK), jnp.float32),
            pltpu.VMEM((CONV_HALO, 2 * F), jnp.float32),
            pltpu.VMEM((T, F), jnp.bfloat16),
        ],
        compiler_params=pltpu.CompilerParams(
            dimension_semantics=("parallel", "arbitrary"), vmem_limit_bytes=V7X_VMEM_LIMIT),
        name="ffn",
    )(x, g, wup, cw, cb, wdn, gf)


def kernel(x, norm_mix_g, w_in, w_pool, pool_scale, w_branch_a, w_branch_b, w_out, norm_ffn_g, w_up, conv_w, conv_b, w_down, norm_final_g):
    depth = w_in.shape[0]
    D = x.shape[-1]
    A = w_branch_a.shape[1]
    P = w_branch_b.shape[1]
    assert x.shape[1] % MOBA_BLOCK == 0 and A % (2 * HEAD_DIM) == 0
    assert w_down.shape[1] % FF_CHUNK == 0 and 2 * POOL_WINDOWS[-1] <= POOL_HALO and POOL_WINDOWS == (2, 4, 8, 16) and CONV_WIDTH - 1 <= CONV_HALO
    bf = jnp.bfloat16
    w_in_bf = w_in.astype(bf)
    wvt = jnp.swapaxes(w_in[:, :, 2 * A:3 * A], 1, 2).astype(bf)
    wa, wp, wb, wo = (w.astype(bf) for w in (w_branch_a, w_pool, w_branch_b, w_out))
    wup, wdn = w_up.astype(bf), w_down.astype(bf)
    g_mix, g_ffn = norm_mix_g.reshape(depth, 1, D), norm_ffn_g.reshape(depth, 1, D)
    ps, cb = pool_scale.reshape(depth, 1, P), conv_b.reshape(depth, 1, -1)
    gf = norm_final_g.reshape(1, D)
    for layer in range(depth):
        q, k, vt, u = _in_proj(x, g_mix, w_in_bf, wvt, layer, A, P)
        a_lo, a_hi = _moba(q, k, vt)
        x = _merge(x, a_lo, a_hi, u, g_mix, w_in_bf, wa, wp, ps, wb, wo, layer)
        x = _ffn(x, g_ffn, wup, conv_w, cb, wdn, gf, layer, layer == depth - 1)
    return x
```

```python
import functools

import jax
import jax.numpy as jnp
from jax import lax
from jax.experimental import pallas as pl
from jax.experimental.pallas import tpu as pltpu

HEAD_DIM = 64
MOBA_BLOCK = 256
MOBA_TOPK = 3
POOL_WINDOWS = (2, 4, 8, 16)
CONV_WIDTH = 3
RMS_EPS = 1e-6
NEG_INF = -1e30
LOG2_E = 1.4426950408889634

ROW_TILE = 1024
FFN_ROW_TILE = 512
SUB_TILE = 256
POOL_HALO = 32
ONES_ROWS = 16
CONV_HALO = 8
FF_CHUNK = 256
V7X_VMEM_LIMIT = 56 * 1024 * 1024

_NT = (((1,), (1,)), ((), ()))


def _layer_operand(block, layer, index=None):
    index = (0,) * len(block) if index is None else index
    return pl.BlockSpec((None,) + tuple(block), lambda b, s: (layer,) + tuple(index),
                        pipeline_mode=pl.Buffered(1))


def _cast_specs(arrays, grid):
    n_steps = grid[0] * grid[1]
    specs, shapes = [], []
    for arr in arrays:
        rows, cols = arr.shape
        assert rows % (16 * n_steps) == 0, "bf16 row blocks are 16-row tiles"
        specs.append(pl.BlockSpec((rows // n_steps, cols), lambda b, s: (b * grid[1] + s, 0)))
        shapes.append(jax.ShapeDtypeStruct(arr.shape, jnp.bfloat16))
    return specs, shapes


def _cast_blocks(in_refs, out_refs):
    for src, dst in zip(in_refs, out_refs, strict=True):
        dst[...] = src[...].astype(dst.dtype)


def _rms_norm(x, g):
    y = x * lax.rsqrt(jnp.mean(x * x, axis=-1, keepdims=True) + RMS_EPS)
    return y * g


def _in_proj_kernel(x_ref, g_ref, wq_ref, wk_ref, wvt_ref, wu_ref, *refs, scale, n_cast):
    cast_in, (q_ref, k_ref, vt_ref, u_ref), cast_out = refs[:n_cast], refs[n_cast:n_cast + 4], refs[n_cast + 4:]
    _cast_blocks(cast_in, cast_out)
    for j in range(vt_ref.shape[1]):
        rows = slice(j * MOBA_BLOCK, (j + 1) * MOBA_BLOCK)
        h = _rms_norm(x_ref[0, rows, :], g_ref[...]).astype(jnp.bfloat16)
        q_ref[0, rows, :] = (jnp.dot(h, wq_ref[...], preferred_element_type=jnp.float32) * scale).astype(q_ref.dtype)
        k_ref[0, rows, :] = jnp.dot(h, wk_ref[...], preferred_element_type=jnp.float32).astype(k_ref.dtype)
        u_ref[0, rows, :] = jnp.dot(h, wu_ref[...], preferred_element_type=jnp.float32).astype(u_ref.dtype)
        vt_ref[0, j] = lax.dot_general(wvt_ref[...], h, _NT,
                                       preferred_element_type=jnp.float32).astype(vt_ref.dtype)


def _in_proj(x, g, w_in, wvt, layer, A, P, casts=()):
    B, S, D = x.shape
    assert P == A, "u is addressed as column block 3 of width A"
    T = min(ROW_TILE, S)
    assert S % T == 0 and T % SUB_TILE == 0
    nb_t = T // MOBA_BLOCK
    nb = S // MOBA_BLOCK
    grid = (B, S // T)
    cast_specs, cast_shapes = _cast_specs(casts, grid)
    return pl.pallas_call(
        functools.partial(_in_proj_kernel, scale=HEAD_DIM ** -0.5 * LOG2_E, n_cast=len(casts)),
        grid=grid,
        in_specs=[
            pl.BlockSpec((1, T, D), lambda b, s: (b, s, 0)),
            _layer_operand((1, D), layer),
            _layer_operand((D, A), layer, (0, 0)),
            _layer_operand((D, A), layer, (0, 1)),
            _layer_operand((A, D), layer),
            _layer_operand((D, P), layer, (0, 3)),
            *cast_specs,
        ],
        out_specs=[
            pl.BlockSpec((1, T, A), lambda b, s: (b, s, 0)),
            pl.BlockSpec((1, T, A), lambda b, s: (b, s, 0)),
            pl.BlockSpec((1, nb_t, A, MOBA_BLOCK), lambda b, s: (b, s, 0, 0)),
            pl.BlockSpec((1, T, P), lambda b, s: (b, s, 0)),
            *cast_specs,
        ],
        out_shape=[
            jax.ShapeDtypeStruct((B, S, A), jnp.bfloat16),
            jax.ShapeDtypeStruct((B, S, A), jnp.bfloat16),
            jax.ShapeDtypeStruct((B, nb, A, MOBA_BLOCK), jnp.bfloat16),
            jax.ShapeDtypeStruct((B, S, P), jnp.bfloat16),
            *cast_shapes,
        ],
        compiler_params=pltpu.CompilerParams(
            dimension_semantics=("parallel", "parallel"), vmem_limit_bytes=V7X_VMEM_LIMIT),
        name="in_proj",
    )(x, g, w_in, w_in, wvt, w_in, *casts)


def _moba_kernel(qa_ref, qb_ref, k_ref, vt_ref, oa_ref, ob_ref, kmean_ref, qza_ref, qzb_ref,
                 s0_ref, mb0_ref, s1_ref, mb1_ref, ota_ref, otb_ref, *, n_heads, n_blocks):
    L = MOBA_BLOCK
    G = 2 * HEAD_DIM
    pair = pl.program_id(1)

    @pl.when(pair == 0)
    def _():
        row = lax.broadcasted_iota(jnp.int32, (n_blocks, n_blocks * L), 0)
        col = lax.broadcasted_iota(jnp.int32, (n_blocks, n_blocks * L), 1)
        ind = jnp.where((col >= row * L) & (col < (row + 1) * L), 1.0 / L, 0.0).astype(jnp.bfloat16)
        k_all = k_ref[0].reshape(n_blocks * L, k_ref.shape[-1])
        kmean_ref[...] = jnp.dot(ind, k_all, preferred_element_type=jnp.float32)

    blk = lax.broadcasted_iota(jnp.int32, (n_blocks, L), 0)
    key_pos = lax.broadcasted_iota(jnp.int32, (L, L), 0)
    qry_pos = lax.broadcasted_iota(jnp.int32, (L, L), 1)
    lane = lax.broadcasted_iota(jnp.int32, (L, G), 1)
    ones_rows = jnp.ones((ONES_ROWS, L), jnp.bfloat16)
    groups = [slice((h // 2) * G, (h // 2 + 1) * G) for h in range(n_heads)]
    slots = ((s0_ref, mb0_ref), (s1_ref, mb1_ref))

    def pv(b, h, p):
        lhs = jnp.concatenate([vt_ref[0, b, h * HEAD_DIM:(h + 1) * HEAD_DIM, :], ones_rows], axis=0)
        return jnp.dot(lhs, p.astype(jnp.bfloat16), preferred_element_type=jnp.float32)

    def stage_a(qz_ref, b, h, slot, causal):
        s_ref, mb_ref = slot
        s = lax.dot_general(k_ref[0, b, :, groups[h]], qz_ref[h], _NT, preferred_element_type=jnp.float32)
        if causal:
            s = jnp.where(key_pos <= qry_pos, s, NEG_INF)
        s_ref[h] = s
        mb_ref[h:h + 1, :] = jnp.max(s, axis=0, keepdims=True)

    def stage_b(b, h, slot, state, on):
        s_ref, mb_ref = slot
        mb = mb_ref[h:h + 1, :]
        if state is None:
            return mb, pv(b, h, jnp.exp2(s_ref[h] - mb))
        m, acc = state
        if on is None:
            m_new = jnp.maximum(m, mb)
            shift = m_new
        else:
            m_new = jnp.where(on, jnp.maximum(m, mb), m)
            shift = jnp.where(on, m_new, -NEG_INF)
        return m_new, jnp.exp2(m - m_new) * acc + pv(b, h, jnp.exp2(s_ref[h] - shift))

    def prepare(ti, q_ref, qz_ref):
        sel = []
        for h in range(n_heads):
            in_head = (lane >= (h % 2) * HEAD_DIM) & (lane < (h % 2 + 1) * HEAD_DIM)
            qz_ref[h] = jnp.where(in_head, q_ref[0, :, groups[h]], jnp.zeros((L, G), q_ref.dtype))
            if ti <= MOBA_TOPK:
                sel.append(None)
                continue
            km = kmean_ref[:, groups[h]]
            km_hi = km.astype(jnp.bfloat16)
            km_lo = (km - km_hi.astype(jnp.float32)).astype(jnp.bfloat16)
            g2 = lax.dot_general(jnp.concatenate([km_hi, km_lo], axis=0), qz_ref[h], _NT,
                                 preferred_element_type=jnp.float32)
            gate = g2[:n_blocks] + g2[n_blocks:]
            rank = jnp.zeros((n_blocks, L), jnp.float32)
            for j in range(ti):
                gj = gate[j:j + 1, :]
                rank = rank + jnp.where((gj > gate) | ((gj == gate) & (blk > j)), 1.0, 0.0)
            sel.append(jnp.where(rank < MOBA_TOPK, 1.0, 0.0))
        return sel

    def finalize(state, ot_ref, o_ref):
        for h in range(n_heads):
            acc = state[h][1]
            inv_l = 1.0 / acc[HEAD_DIM:HEAD_DIM + 1, :]
            ot_ref[h * HEAD_DIM:(h + 1) * HEAD_DIM, :] = (acc[0:HEAD_DIM, :] * inv_l).astype(ot_ref.dtype)
        eye = jnp.where(key_pos == qry_pos, 1.0, 0.0).astype(jnp.bfloat16)
        o_ref[0] = lax.dot_general(eye, ot_ref[...], _NT, preferred_element_type=jnp.float32).astype(o_ref.dtype)

    def tile_pair(tp):
        tiles = ((tp, qa_ref, qza_ref, ota_ref, oa_ref), (n_blocks - 1 - tp, qb_ref, qzb_ref, otb_ref, ob_ref))
        sels = [prepare(ti, q_ref, qz_ref) for ti, q_ref, qz_ref, _, _ in tiles]
        visits = [(w, b, n == 0, n == tiles[w][0]) for w in range(2)
                  for n, b in enumerate([tiles[w][0]] + list(range(tiles[w][0])))]
        states = [[None] * n_heads, [None] * n_heads]
        for h in range(n_heads):
            stage_a(tiles[0][2], visits[0][1], h, slots[0], causal=True)
        for n, (w, b, first, last) in enumerate(visits):
            for h in range(n_heads):
                if n + 1 < len(visits):
                    w2, b2, first2, _ = visits[n + 1]
                    stage_a(tiles[w2][2], b2, h, slots[(n + 1) % 2], causal=first2)
                on = None if (first or sels[w][h] is None) else sels[w][h][b:b + 1, :] > 0.0
                states[w][h] = stage_b(b, h, slots[n % 2], states[w][h], on)
            if last:
                finalize(states[w], tiles[w][3], tiles[w][4])

    for tp in range(n_blocks // 2):
        pl.when(pair == tp)(functools.partial(tile_pair, tp))


def _moba(q, k, vt):
    B, S, A = q.shape
    L = MOBA_BLOCK
    nb = S // L
    H = A // HEAD_DIM
    assert nb % 2 == 0
    k4 = k.reshape(B, nb, L, A)
    half = jax.ShapeDtypeStruct((B, S // 2, A), jnp.bfloat16)
    return pl.pallas_call(
        functools.partial(_moba_kernel, n_heads=H, n_blocks=nb),
        grid=(B, nb // 2),
        in_specs=[
            pl.BlockSpec((1, L, A), lambda b, j: (b, j, 0)),
            pl.BlockSpec((1, L, A), lambda b, j: (b, nb - 1 - j, 0)),
            pl.BlockSpec((1, nb, L, A), lambda b, j: (b, 0, 0, 0)),
            pl.BlockSpec((1, nb, A, L), lambda b, j: (b, 0, 0, 0)),
        ],
        out_specs=[
            pl.BlockSpec((1, L, A), lambda b, j: (b, j, 0)),
            pl.BlockSpec((1, L, A), lambda b, j: (b, nb // 2 - 1 - j, 0)),
        ],
        out_shape=[half, half],
        scratch_shapes=[
            pltpu.VMEM((nb, A), jnp.float32),
            pltpu.VMEM((H, L, 2 * HEAD_DIM), jnp.bfloat16),
            pltpu.VMEM((H, L, 2 * HEAD_DIM), jnp.bfloat16),
            pltpu.VMEM((H, L, L), jnp.float32),
            pltpu.VMEM((H, L), jnp.float32),
            pltpu.VMEM((H, L, L), jnp.float32),
            pltpu.VMEM((H, L), jnp.float32),
            pltpu.VMEM((A, L), jnp.bfloat16),
            pltpu.VMEM((A, L), jnp.bfloat16),
        ],
        compiler_params=pltpu.CompilerParams(
            dimension_semantics=("parallel", "arbitrary"), vmem_limit_bytes=V7X_VMEM_LIMIT),
        name="moba",
    )(q, q, k4, vt)


def _sigmoid(x):
    return 0.5 * jnp.tanh(0.5 * x) + 0.5


def _merge_kernel(x_ref, a_lo_ref, a_hi_ref, u_ref, g_ref, wg_ref, wa_ref, wp_ref, ps_ref, wb_ref, wo_ref,
                  *refs, n_lo, n_cast):
    cast_in, o_ref, cast_out = refs[:n_cast], refs[n_cast], refs[n_cast + 1:2 * n_cast + 1]
    uext_ref, w2_ref, w4_ref, w8_ref = refs[2 * n_cast + 1:]
    _cast_blocks(cast_in, cast_out)
    T = x_ref.shape[1]
    D = x_ref.shape[2]
    GW = wp_ref.shape[1]
    R = SUB_TILE
    HALO = POOL_HALO
    s_idx = pl.program_id(1)

    @pl.when(s_idx == 0)
    def _():
        uext_ref[0:HALO, :] = jnp.zeros((HALO, uext_ref.shape[1]), jnp.float32)

    def branch_in(r0):
        a = jnp.where(s_idx < n_lo, a_lo_ref[0, r0:r0 + R, :], a_hi_ref[0, r0:r0 + R, :])
        y_a = jnp.dot(a, wa_ref[...], preferred_element_type=jnp.float32)
        h = _rms_norm(x_ref[0, r0:r0 + R, :], g_ref[...]).astype(jnp.bfloat16)
        return y_a, jnp.dot(h, wg_ref[...], preferred_element_type=jnp.float32)

    def window_sums():
        n = HALO + T
        uext_ref[HALO:n, :] = u_ref[0].astype(jnp.float32)
        w2_ref[8:n, :] = uext_ref[8:n, :] + uext_ref[7:n - 1, :]
        w4_ref[16:n, GW:] = w2_ref[16:n, GW:] + w2_ref[14:n - 2, GW:]
        w8_ref[24:n, 2 * GW:] = w4_ref[24:n, 2 * GW:] + w4_ref[20:n - 4, 2 * GW:]

    def branch_out(r0, y_a, g_pre):
        e0 = HALO + r0
        wsums = (w2_ref[e0:e0 + R, 0:GW], w4_ref[e0:e0 + R, GW:2 * GW], w8_ref[e0:e0 + R, 2 * GW:3 * GW],
                 w8_ref[e0:e0 + R, 3 * GW:] + w8_ref[e0 - 8:e0 - 8 + R, 3 * GW:])
        t_pos = s_idx * T + r0 + lax.broadcasted_iota(jnp.int32, (R, GW), 0)
        ys = []
        for gi, win in enumerate(POOL_WINDOWS):
            count = jnp.minimum(t_pos + 1, win).astype(jnp.float32)
            mixed = (wsums[gi] / count - uext_ref[e0:e0 + R, gi * GW:(gi + 1) * GW]).astype(jnp.bfloat16)
            ys.append(jnp.dot(mixed, wp_ref[gi], preferred_element_type=jnp.float32))
        pooled = (jnp.concatenate(ys, axis=1) * ps_ref[...]).astype(jnp.bfloat16)
        y_b = jnp.dot(pooled, wb_ref[...], preferred_element_type=jnp.float32)
        gates = _sigmoid(g_pre)
        merged = (gates[:, :D] * y_a + gates[:, D:] * y_b).astype(jnp.bfloat16)
        o_ref[0, r0:r0 + R, :] = x_ref[0, r0:r0 + R, :] + jnp.dot(merged, wo_ref[...],
                                                                    preferred_element_type=jnp.float32)

    n_sub = T // R
    pending = branch_in(0)
    window_sums()
    for j in range(n_sub):
        nxt = branch_in((j + 1) * R) if j + 1 < n_sub else None
        branch_out(j * R, *pending)
        pending = nxt
    uext_ref[0:HALO, :] = uext_ref[T:T + HALO, :]


def _merge(x, a_lo, a_hi, u, g, w_in, wa, wp, ps, wb, wo, layer, casts=()):
    B, S, D = x.shape
    A = a_lo.shape[2]
    P = u.shape[2]
    assert 3 * A + P == 2 * D, "the branch gates are addressed as column block 1 of width 2D"
    T = min(ROW_TILE, S)
    assert S % T == 0 and T % SUB_TILE == 0
    row = lambda b, s: (b, s, 0)
    n_lo = S // T // 2
    assert S % (2 * T) == 0
    grid = (B, S // T)
    cast_specs, cast_shapes = _cast_specs(casts, grid)
    return pl.pallas_call(
        functools.partial(_merge_kernel, n_lo=n_lo, n_cast=len(casts)),
        grid=grid,
        in_specs=[
            pl.BlockSpec((1, T, D), row),
            pl.BlockSpec((1, T, A), lambda b, s: (b, jnp.minimum(s, n_lo - 1), 0)),
            pl.BlockSpec((1, T, A), lambda b, s: (b, jnp.maximum(s - n_lo, 0), 0)),
            pl.BlockSpec((1, T, P), row),
            _layer_operand((1, D), layer),
            _layer_operand((D, 2 * D), layer, (0, 1)),
            _layer_operand(wa.shape[1:], layer),
            _layer_operand(wp.shape[1:], layer),
            _layer_operand((1, P), layer),
            _layer_operand(wb.shape[1:], layer),
            _layer_operand(wo.shape[1:], layer),
            *cast_specs,
        ],
        out_specs=[pl.BlockSpec((1, T, D), row), *cast_specs],
        out_shape=[jax.ShapeDtypeStruct((B, S, D), x.dtype), *cast_shapes],
        scratch_shapes=[pltpu.VMEM((POOL_HALO + T, P), jnp.float32)] * 4,
        compiler_params=pltpu.CompilerParams(
            dimension_semantics=("parallel", "arbitrary"), vmem_limit_bytes=V7X_VMEM_LIMIT),
        name="merge",
    )(x, a_lo, a_hi, u, g, w_in, wa, wp, ps, wb, wo, *casts)


def _ffn_kernel(x_ref, g_ref, wup_ref, cw_ref, cb_ref, wdn_ref, gf_ref, o_ref,
                ext_ref, carry_ref, act_ref, *, final_norm):
    T = x_ref.shape[1]
    F = wdn_ref.shape[0]
    FC = FF_CHUNK
    H = CONV_HALO
    s_idx = pl.program_id(1)

    @pl.when(s_idx == 0)
    def _():
        carry_ref[...] = jnp.zeros(carry_ref.shape, jnp.float32)

    h = _rms_norm(x_ref[0], g_ref[...]).astype(jnp.bfloat16)

    def conv_cols(c0, scale):
        a = jnp.dot(h, wup_ref[:, c0:c0 + FC], preferred_element_type=jnp.float32)
        ext_ref[0:H, :] = carry_ref[:, c0:c0 + FC]
        ext_ref[H:H + T, :] = a
        carry_ref[:, c0:c0 + FC] = a[T - H:, :]
        w = cw_ref[:, c0:c0 + FC] * scale
        return (cb_ref[:, c0:c0 + FC] * scale + w[2:3, :] * a
                + w[1:2, :] * ext_ref[H - 1:H - 1 + T, :]
                + w[0:1, :] * ext_ref[H - 2:H - 2 + T, :])

    for c in range(F // FC):
        half_gate = conv_cols(c * FC, 0.5)
        val = conv_cols(F + c * FC, 1.0)
        act_ref[:, c * FC:(c + 1) * FC] = (half_gate * (jnp.tanh(half_gate) + 1.0) * val).astype(act_ref.dtype)

    y = x_ref[0] + jnp.dot(act_ref[...], wdn_ref[...], preferred_element_type=jnp.float32)
    if final_norm:
        y = _rms_norm(y, gf_ref[...])
    o_ref[0] = y


def _ffn(x, g, wup, cw, cb, wdn, gf, layer, final_norm):
    B, S, D = x.shape
    F = wdn.shape[1]
    T = min(FFN_ROW_TILE, S)
    assert S % T == 0
    row = lambda b, s: (b, s, 0)
    return pl.pallas_call(
        functools.partial(_ffn_kernel, final_norm=final_norm),
        grid=(B, S // T),
        in_specs=[
            pl.BlockSpec((1, T, D), row),
            _layer_operand((1, D), layer),
            _layer_operand(wup.shape[1:], layer),
            _layer_operand(cw.shape[1:], layer),
            _layer_operand(cb.shape[1:], layer),
            _layer_operand(wdn.shape[1:], layer),
            pl.BlockSpec((1, D), lambda b, s: (0, 0), pipeline_mode=pl.Buffered(1)),
        ],
        out_specs=pl.BlockSpec((1, T, D), row),
        out_shape=jax.ShapeDtypeStruct((B, S, D), x.dtype),
        scratch_shapes=[
            pltpu.VMEM((CONV_HALO + T, FF_CHUNK), jnp.float32),
            pltpu.VMEM((CONV_HALO, 2 * F), jnp.float32),
            pltpu.VMEM((T, F), jnp.bfloat16),
        ],
        compiler_params=pltpu.CompilerParams(
            dimension_semantics=("parallel", "arbitrary"), vmem_limit_bytes=V7X_VMEM_LIMIT),
        name="ffn",
    )(x, g, wup, cw, cb, wdn, gf)


def kernel(x, norm_mix_g, w_in, w_pool, pool_scale, w_branch_a, w_branch_b, w_out, norm_ffn_g, w_up, conv_w, conv_b, w_down, norm_final_g):
    depth = w_in.shape[0]
    D = x.shape[-1]
    A = w_branch_a.shape[1]
    P = w_branch_b.shape[1]
    assert x.shape[1] % MOBA_BLOCK == 0 and A % (2 * HEAD_DIM) == 0
    assert w_down.shape[1] % FF_CHUNK == 0 and 2 * POOL_WINDOWS[-1] <= POOL_HALO and POOL_WINDOWS == (2, 4, 8, 16) and CONV_WIDTH - 1 <= CONV_HALO
    bf = jnp.bfloat16
    w_in_bf = w_in.astype(bf)
    wvt = jnp.swapaxes(w_in[:, :, 2 * A:3 * A], 1, 2).astype(bf)
    merge_weights = (w_branch_a, w_pool, w_branch_b, w_out)
    ffn_weights = (w_up, w_down)
    flat = lambda ws: tuple(w.reshape(-1, w.shape[-1]) for w in ws)
    g_mix, g_ffn = norm_mix_g.reshape(depth, 1, D), norm_ffn_g.reshape(depth, 1, D)
    ps, cb = pool_scale.reshape(depth, 1, P), conv_b.reshape(depth, 1, -1)
    gf = norm_final_g.reshape(1, D)
    for layer in range(depth):
        q, k, vt, u, *cast = _in_proj(x, g_mix, w_in_bf, wvt, layer, A, P, casts=flat(merge_weights) if layer == 0 else ())
        if layer == 0:
            wa, wp, wb, wo = (c.reshape(w.shape) for c, w in zip(cast, merge_weights))
        a_lo, a_hi = _moba(q, k, vt)
        x, *cast = _merge(x, a_lo, a_hi, u, g_mix, w_in_bf, wa, wp, ps, wb, wo, layer,
                          casts=flat(ffn_weights) if layer == 0 else ())
        if layer == 0:
            wup, wdn = (c.reshape(w.shape) for c, w in zip(cast, ffn_weights))
        x = _ffn(x, g_ffn, wup, conv_w, cb, wdn, gf, layer, layer == depth - 1)
    return x
```

```python
import functools

import jax
import jax.numpy as jnp
from jax import lax
from jax.experimental import pallas as pl
from jax.experimental.pallas import tpu as pltpu

HEAD_DIM = 64
MOBA_BLOCK = 256
MOBA_TOPK = 3
POOL_WINDOWS = (2, 4, 8, 16)
CONV_WIDTH = 3
RMS_EPS = 1e-6
NEG_INF = -1e30
LOG2_E = 1.4426950408889634

ROW_TILE = 1024
FFN_ROW_TILE = 512
SUB_TILE = 256
POOL_HALO = 32
ONES_ROWS = 16
CONV_HALO = 8
FF_CHUNK = 256
V7X_VMEM_LIMIT = 56 * 1024 * 1024

_NT = (((1,), (1,)), ((), ()))


def _layer_operand(block, layer, index=None):
    index = (0,) * len(block) if index is None else index
    return pl.BlockSpec((None,) + tuple(block), lambda b, s: (layer,) + tuple(index),
                        pipeline_mode=pl.Buffered(1))


def _whole(shape):
    return pl.BlockSpec(shape, lambda b, s: (0,) * len(shape), pipeline_mode=pl.Buffered(1))


def _cast_specs(arrays, grid):
    n_steps = grid[0] * grid[1]
    specs, shapes = [], []
    for arr in arrays:
        rows, cols = arr.shape
        assert rows % (16 * n_steps) == 0, "bf16 row blocks are 16-row tiles"
        specs.append(pl.BlockSpec((rows // n_steps, cols), lambda b, s: (b * grid[1] + s, 0)))
        shapes.append(jax.ShapeDtypeStruct(arr.shape, jnp.bfloat16))
    return specs, shapes


def _cast_blocks(in_refs, out_refs):
    for src, dst in zip(in_refs, out_refs, strict=True):
        dst[...] = src[...].astype(dst.dtype)


def _rms_norm(x, g):
    y = x * lax.rsqrt(jnp.mean(x * x, axis=-1, keepdims=True) + RMS_EPS)
    return y * g


def _in_proj_kernel(x_ref, g_ref, wq_ref, wk_ref, wvt_ref, wu_ref, *refs, scale, layer, n_cast):
    cast_in, (q_ref, k_ref, vt_ref, u_ref), cast_out = refs[:n_cast], refs[n_cast:n_cast + 4], refs[n_cast + 4:]
    _cast_blocks(cast_in, cast_out)
    for j in range(vt_ref.shape[1]):
        rows = slice(j * MOBA_BLOCK, (j + 1) * MOBA_BLOCK)
        h = _rms_norm(x_ref[0, rows, :], g_ref[layer:layer + 1, :]).astype(jnp.bfloat16)
        q_ref[0, rows, :] = (jnp.dot(h, wq_ref[...], preferred_element_type=jnp.float32) * scale).astype(q_ref.dtype)
        k_ref[0, rows, :] = jnp.dot(h, wk_ref[...], preferred_element_type=jnp.float32).astype(k_ref.dtype)
        u_ref[0, rows, :] = jnp.dot(h, wu_ref[...], preferred_element_type=jnp.float32).astype(u_ref.dtype)
        vt_ref[0, j] = lax.dot_general(wvt_ref[...], h, _NT,
                                       preferred_element_type=jnp.float32).astype(vt_ref.dtype)


def _in_proj(x, g, w_in, w_layer, wvt, layer, A, P, casts=()):
    B, S, D = x.shape
    assert P == A, "u is addressed as column block 3 of width A"
    T = min(ROW_TILE, S)
    assert S % T == 0 and T % SUB_TILE == 0
    nb_t = T // MOBA_BLOCK
    nb = S // MOBA_BLOCK
    grid = (B, S // T)
    cast_specs, cast_shapes = _cast_specs(casts, grid)
    return pl.pallas_call(
        functools.partial(_in_proj_kernel, scale=HEAD_DIM ** -0.5 * LOG2_E, layer=layer, n_cast=len(casts)),
        grid=grid,
        in_specs=[
            pl.BlockSpec((1, T, D), lambda b, s: (b, s, 0)),
            _whole(g.shape),
            _layer_operand((D, A), w_layer, (0, 0)),
            _layer_operand((D, A), w_layer, (0, 1)),
            _layer_operand((A, D), layer),
            _layer_operand((D, P), w_layer, (0, 3)),
            *cast_specs,
        ],
        out_specs=[
            pl.BlockSpec((1, T, A), lambda b, s: (b, s, 0)),
            pl.BlockSpec((1, T, A), lambda b, s: (b, s, 0)),
            pl.BlockSpec((1, nb_t, A, MOBA_BLOCK), lambda b, s: (b, s, 0, 0)),
            pl.BlockSpec((1, T, P), lambda b, s: (b, s, 0)),
            *cast_specs,
        ],
        out_shape=[
            jax.ShapeDtypeStruct((B, S, A), jnp.bfloat16),
            jax.ShapeDtypeStruct((B, S, A), jnp.bfloat16),
            jax.ShapeDtypeStruct((B, nb, A, MOBA_BLOCK), jnp.bfloat16),
            jax.ShapeDtypeStruct((B, S, P), jnp.bfloat16),
            *cast_shapes,
        ],
        compiler_params=pltpu.CompilerParams(
            dimension_semantics=("parallel", "parallel"), vmem_limit_bytes=V7X_VMEM_LIMIT),
        name="in_proj",
    )(x, g, w_in, w_in, wvt, w_in, *casts)


def _moba_kernel(qa_ref, qb_ref, k_ref, vt_ref, oa_ref, ob_ref, kmean_ref, qza_ref, qzb_ref,
                 s0_ref, mb0_ref, s1_ref, mb1_ref, ota_ref, otb_ref, *, n_heads, n_blocks):
    L = MOBA_BLOCK
    G = 2 * HEAD_DIM
    pair = pl.program_id(1)

    @pl.when(pair == 0)
    def _():
        row = lax.broadcasted_iota(jnp.int32, (n_blocks, n_blocks * L), 0)
        col = lax.broadcasted_iota(jnp.int32, (n_blocks, n_blocks * L), 1)
        ind = jnp.where((col >= row * L) & (col < (row + 1) * L), 1.0 / L, 0.0).astype(jnp.bfloat16)
        k_all = k_ref[0].reshape(n_blocks * L, k_ref.shape[-1])
        kmean_ref[...] = jnp.dot(ind, k_all, preferred_element_type=jnp.float32)

    blk = lax.broadcasted_iota(jnp.int32, (n_blocks, L), 0)
    key_pos = lax.broadcasted_iota(jnp.int32, (L, L), 0)
    qry_pos = lax.broadcasted_iota(jnp.int32, (L, L), 1)
    lane = lax.broadcasted_iota(jnp.int32, (L, G), 1)
    ones_rows = jnp.ones((ONES_ROWS, L), jnp.bfloat16)
    groups = [slice((h // 2) * G, (h // 2 + 1) * G) for h in range(n_heads)]
    slots = ((s0_ref, mb0_ref), (s1_ref, mb1_ref))

    def pv(b, h, p):
        lhs = jnp.concatenate([vt_ref[0, b, h * HEAD_DIM:(h + 1) * HEAD_DIM, :], ones_rows], axis=0)
        return jnp.dot(lhs, p.astype(jnp.bfloat16), preferred_element_type=jnp.float32)

    def stage_a(qz_ref, b, h, slot, causal):
        s_ref, mb_ref = slot
        s = lax.dot_general(k_ref[0, b, :, groups[h]], qz_ref[h], _NT, preferred_element_type=jnp.float32)
        if causal:
            s = jnp.where(key_pos <= qry_pos, s, NEG_INF)
        s_ref[h] = s
        mb_ref[h:h + 1, :] = jnp.max(s, axis=0, keepdims=True)

    def stage_b(b, h, slot, state, on):
        s_ref, mb_ref = slot
        mb = mb_ref[h:h + 1, :]
        if state is None:
            return mb, pv(b, h, jnp.exp2(s_ref[h] - mb))
        m, acc = state
        if on is None:
            m_new = jnp.maximum(m, mb)
            shift = m_new
        else:
            m_new = jnp.where(on, jnp.maximum(m, mb), m)
            shift = jnp.where(on, m_new, -NEG_INF)
        return m_new, jnp.exp2(m - m_new) * acc + pv(b, h, jnp.exp2(s_ref[h] - shift))

    def prepare(ti, q_ref, qz_ref):
        sel = []
        for h in range(n_heads):
            in_head = (lane >= (h % 2) * HEAD_DIM) & (lane < (h % 2 + 1) * HEAD_DIM)
            qz_ref[h] = jnp.where(in_head, q_ref[0, :, groups[h]], jnp.zeros((L, G), q_ref.dtype))
            if ti <= MOBA_TOPK:
                sel.append(None)
                continue
            km = kmean_ref[:, groups[h]]
            km_hi = km.astype(jnp.bfloat16)
            km_lo = (km - km_hi.astype(jnp.float32)).astype(jnp.bfloat16)
            g2 = lax.dot_general(jnp.concatenate([km_hi, km_lo], axis=0), qz_ref[h], _NT,
                                 preferred_element_type=jnp.float32)
            gate = g2[:n_blocks] + g2[n_blocks:]
            rank = jnp.zeros((n_blocks, L), jnp.float32)
            for j in range(ti):
                gj = gate[j:j + 1, :]
                rank = rank + jnp.where((gj > gate) | ((gj == gate) & (blk > j)), 1.0, 0.0)
            sel.append(jnp.where(rank < MOBA_TOPK, 1.0, 0.0))
        return sel

    def finalize(state, ot_ref, o_ref):
        for h in range(n_heads):
            acc = state[h][1]
            inv_l = 1.0 / acc[HEAD_DIM:HEAD_DIM + 1, :]
            ot_ref[h * HEAD_DIM:(h + 1) * HEAD_DIM, :] = (acc[0:HEAD_DIM, :] * inv_l).astype(ot_ref.dtype)
        eye = jnp.where(key_pos == qry_pos, 1.0, 0.0).astype(jnp.bfloat16)
        o_ref[0] = lax.dot_general(eye, ot_ref[...], _NT, preferred_element_type=jnp.float32).astype(o_ref.dtype)

    def tile_pair(tp):
        tiles = ((tp, qa_ref, qza_ref, ota_ref, oa_ref), (n_blocks - 1 - tp, qb_ref, qzb_ref, otb_ref, ob_ref))
        sels = [prepare(ti, q_ref, qz_ref) for ti, q_ref, qz_ref, _, _ in tiles]
        visits = [(w, b, n == 0, n == tiles[w][0]) for w in range(2)
                  for n, b in enumerate([tiles[w][0]] + list(range(tiles[w][0])))]
        states = [[None] * n_heads, [None] * n_heads]
        for h in range(n_heads):
            stage_a(tiles[0][2], visits[0][1], h, slots[0], causal=True)
        for n, (w, b, first, last) in enumerate(visits):
            for h in range(n_heads):
                if n + 1 < len(visits):
                    w2, b2, first2, _ = visits[n + 1]
                    stage_a(tiles[w2][2], b2, h, slots[(n + 1) % 2], causal=first2)
                on = None if (first or sels[w][h] is None) else sels[w][h][b:b + 1, :] > 0.0
                states[w][h] = stage_b(b, h, slots[n % 2], states[w][h], on)
            if last:
                finalize(states[w], tiles[w][3], tiles[w][4])

    for tp in range(n_blocks // 2):
        pl.when(pair == tp)(functools.partial(tile_pair, tp))


def _moba(q, k, vt):
    B, S, A = q.shape
    L = MOBA_BLOCK
    nb = S // L
    H = A // HEAD_DIM
    assert nb % 2 == 0
    k4 = k.reshape(B, nb, L, A)
    half = jax.ShapeDtypeStruct((B, S // 2, A), jnp.bfloat16)
    return pl.pallas_call(
        functools.partial(_moba_kernel, n_heads=H, n_blocks=nb),
        grid=(B, nb // 2),
        in_specs=[
            pl.BlockSpec((1, L, A), lambda b, j: (b, j, 0)),
            pl.BlockSpec((1, L, A), lambda b, j: (b, nb - 1 - j, 0)),
            pl.BlockSpec((1, nb, L, A), lambda b, j: (b, 0, 0, 0)),
            pl.BlockSpec((1, nb, A, L), lambda b, j: (b, 0, 0, 0)),
        ],
        out_specs=[
            pl.BlockSpec((1, L, A), lambda b, j: (b, j, 0)),
            pl.BlockSpec((1, L, A), lambda b, j: (b, nb // 2 - 1 - j, 0)),
        ],
        out_shape=[half, half],
        scratch_shapes=[
            pltpu.VMEM((nb, A), jnp.float32),
            pltpu.VMEM((H, L, 2 * HEAD_DIM), jnp.bfloat16),
            pltpu.VMEM((H, L, 2 * HEAD_DIM), jnp.bfloat16),
            pltpu.VMEM((H, L, L), jnp.float32),
            pltpu.VMEM((H, L), jnp.float32),
            pltpu.VMEM((H, L, L), jnp.float32),
            pltpu.VMEM((H, L), jnp.float32),
            pltpu.VMEM((A, L), jnp.bfloat16),
            pltpu.VMEM((A, L), jnp.bfloat16),
        ],
        compiler_params=pltpu.CompilerParams(
            dimension_semantics=("parallel", "arbitrary"), vmem_limit_bytes=V7X_VMEM_LIMIT),
        name="moba",
    )(q, q, k4, vt)


def _sigmoid(x):
    return 0.5 * jnp.tanh(0.5 * x) + 0.5


def _merge_kernel(x_ref, a_lo_ref, a_hi_ref, u_ref, g_ref, wg_ref, wa_ref, wp_ref, ps_ref, wb_ref, wo_ref,
                  *refs, layer, n_lo, n_cast):
    cast_in, o_ref, cast_out = refs[:n_cast], refs[n_cast], refs[n_cast + 1:2 * n_cast + 1]
    uext_ref, w2_ref, w4_ref, w8_ref = refs[2 * n_cast + 1:]
    _cast_blocks(cast_in, cast_out)
    T = x_ref.shape[1]
    D = x_ref.shape[2]
    GW = wp_ref.shape[1]
    R = SUB_TILE
    HALO = POOL_HALO
    s_idx = pl.program_id(1)

    @pl.when(s_idx == 0)
    def _():
        uext_ref[0:HALO, :] = jnp.zeros((HALO, uext_ref.shape[1]), jnp.float32)

    def branch_in(r0):
        a = jnp.where(s_idx < n_lo, a_lo_ref[0, r0:r0 + R, :], a_hi_ref[0, r0:r0 + R, :])
        y_a = jnp.dot(a, wa_ref[...], preferred_element_type=jnp.float32)
        h = _rms_norm(x_ref[0, r0:r0 + R, :], g_ref[layer:layer + 1, :]).astype(jnp.bfloat16)
        return y_a, jnp.dot(h, wg_ref[...], preferred_element_type=jnp.float32)

    def window_sums():
        n = HALO + T
        uext_ref[HALO:n, :] = u_ref[0].astype(jnp.float32)
        w2_ref[8:n, :] = uext_ref[8:n, :] + uext_ref[7:n - 1, :]
        w4_ref[16:n, GW:] = w2_ref[16:n, GW:] + w2_ref[14:n - 2, GW:]
        w8_ref[24:n, 2 * GW:] = w4_ref[24:n, 2 * GW:] + w4_ref[20:n - 4, 2 * GW:]

    def branch_out(r0, y_a, g_pre):
        e0 = HALO + r0
        wsums = (w2_ref[e0:e0 + R, 0:GW], w4_ref[e0:e0 + R, GW:2 * GW], w8_ref[e0:e0 + R, 2 * GW:3 * GW],
                 w8_ref[e0:e0 + R, 3 * GW:] + w8_ref[e0 - 8:e0 - 8 + R, 3 * GW:])
        t_pos = s_idx * T + r0 + lax.broadcasted_iota(jnp.int32, (R, GW), 0)
        ys = []
        for gi, win in enumerate(POOL_WINDOWS):
            count = jnp.minimum(t_pos + 1, win).astype(jnp.float32)
            mixed = (wsums[gi] / count - uext_ref[e0:e0 + R, gi * GW:(gi + 1) * GW]).astype(jnp.bfloat16)
            ys.append(jnp.dot(mixed, wp_ref[gi], preferred_element_type=jnp.float32))
        pooled = (jnp.concatenate(ys, axis=1) * ps_ref[layer:layer + 1, :]).astype(jnp.bfloat16)
        y_b = jnp.dot(pooled, wb_ref[...], preferred_element_type=jnp.float32)
        gates = _sigmoid(g_pre)
        merged = (gates[:, :D] * y_a + gates[:, D:] * y_b).astype(jnp.bfloat16)
        o_ref[0, r0:r0 + R, :] = x_ref[0, r0:r0 + R, :] + jnp.dot(merged, wo_ref[...],
                                                                    preferred_element_type=jnp.float32)

    n_sub = T // R
    pending = branch_in(0)
    window_sums()
    for j in range(n_sub):
        nxt = branch_in((j + 1) * R) if j + 1 < n_sub else None
        branch_out(j * R, *pending)
        pending = nxt
    uext_ref[0:HALO, :] = uext_ref[T:T + HALO, :]


def _merge(x, a_lo, a_hi, u, g, w_in, w_layer, wa, wp, ps, wb, wo, layer, casts=()):
    B, S, D = x.shape
    A = a_lo.shape[2]
    P = u.shape[2]
    assert 3 * A + P == 2 * D, "the branch gates are addressed as column block 1 of width 2D"
    T = min(ROW_TILE, S)
    assert S % T == 0 and T % SUB_TILE == 0
    row = lambda b, s: (b, s, 0)
    n_lo = S // T // 2
    assert S % (2 * T) == 0
    grid = (B, S // T)
    cast_specs, cast_shapes = _cast_specs(casts, grid)
    return pl.pallas_call(
        functools.partial(_merge_kernel, layer=layer, n_lo=n_lo, n_cast=len(casts)),
        grid=grid,
        in_specs=[
            pl.BlockSpec((1, T, D), row),
            pl.BlockSpec((1, T, A), lambda b, s: (b, jnp.minimum(s, n_lo - 1), 0)),
            pl.BlockSpec((1, T, A), lambda b, s: (b, jnp.maximum(s - n_lo, 0), 0)),
            pl.BlockSpec((1, T, P), row),
            _whole(g.shape),
            _layer_operand((D, 2 * D), w_layer, (0, 1)),
            _layer_operand(wa.shape[1:], layer),
            _layer_operand(wp.shape[1:], layer),
            _whole(ps.shape),
            _layer_operand(wb.shape[1:], layer),
            _layer_operand(wo.shape[1:], layer),
            *cast_specs,
        ],
        out_specs=[pl.BlockSpec((1, T, D), row), *cast_specs],
        out_shape=[jax.ShapeDtypeStruct((B, S, D), x.dtype), *cast_shapes],
        scratch_shapes=[pltpu.VMEM((POOL_HALO + T, P), jnp.float32)] * 4,
        compiler_params=pltpu.CompilerParams(
            dimension_semantics=("parallel", "arbitrary"), vmem_limit_bytes=V7X_VMEM_LIMIT),
        name="merge",
    )(x, a_lo, a_hi, u, g, w_in, wa, wp, ps, wb, wo, *casts)


def _ffn_kernel(x_ref, g_ref, wup_ref, cw_ref, cb_ref, wdn_ref, gf_ref, o_ref,
                ext_ref, carry_ref, act_ref, *, layer, final_norm):
    T = x_ref.shape[1]
    F = wdn_ref.shape[0]
    FC = FF_CHUNK
    H = CONV_HALO
    s_idx = pl.program_id(1)

    @pl.when(s_idx == 0)
    def _():
        carry_ref[...] = jnp.zeros(carry_ref.shape, jnp.float32)

    h = _rms_norm(x_ref[0], g_ref[layer:layer + 1, :]).astype(jnp.bfloat16)

    def conv_cols(c0, scale):
        a = jnp.dot(h, wup_ref[:, c0:c0 + FC], preferred_element_type=jnp.float32)
        ext_ref[0:H, :] = carry_ref[:, c0:c0 + FC]
        ext_ref[H:H + T, :] = a
        carry_ref[:, c0:c0 + FC] = a[T - H:, :]
        w = cw_ref[:, c0:c0 + FC] * scale
        return (cb_ref[layer:layer + 1, c0:c0 + FC] * scale + w[2:3, :] * a
                + w[1:2, :] * ext_ref[H - 1:H - 1 + T, :]
                + w[0:1, :] * ext_ref[H - 2:H - 2 + T, :])

    for c in range(F // FC):
        half_gate = conv_cols(c * FC, 0.5)
        val = conv_cols(F + c * FC, 1.0)
        act_ref[:, c * FC:(c + 1) * FC] = (half_gate * (jnp.tanh(half_gate) + 1.0) * val).astype(act_ref.dtype)

    y = x_ref[0] + jnp.dot(act_ref[...], wdn_ref[...], preferred_element_type=jnp.float32)
    if final_norm:
        y = _rms_norm(y, gf_ref[...])
    o_ref[0] = y


def _ffn(x, g, wup, cw, cb, wdn, gf, layer, final_norm):
    B, S, D = x.shape
    F = wdn.shape[1]
    T = min(FFN_ROW_TILE, S)
    assert S % T == 0
    row = lambda b, s: (b, s, 0)
    return pl.pallas_call(
        functools.partial(_ffn_kernel, layer=layer, final_norm=final_norm),
        grid=(B, S // T),
        in_specs=[
            pl.BlockSpec((1, T, D), row),
            _whole(g.shape),
            _layer_operand(wup.shape[1:], layer),
            _layer_operand(cw.shape[1:], layer),
            _whole(cb.shape),
            _layer_operand(wdn.shape[1:], layer),
            _whole(gf.shape),
        ],
        out_specs=pl.BlockSpec((1, T, D), row),
        out_shape=jax.ShapeDtypeStruct((B, S, D), x.dtype),
        scratch_shapes=[
            pltpu.VMEM((CONV_HALO + T, FF_CHUNK), jnp.float32),
            pltpu.VMEM((CONV_HALO, 2 * F), jnp.float32),
            pltpu.VMEM((T, F), jnp.bfloat16),
        ],
        compiler_params=pltpu.CompilerParams(
            dimension_semantics=("parallel", "arbitrary"), vmem_limit_bytes=V7X_VMEM_LIMIT),
        name="ffn",
    )(x, g, wup, cw, cb, wdn, gf)


def kernel(x, norm_mix_g, w_in, w_pool, pool_scale, w_branch_a, w_branch_b, w_out, norm_ffn_g, w_up, conv_w, conv_b, w_down, norm_final_g):
    depth = w_in.shape[0]
    D = x.shape[-1]
    A = w_branch_a.shape[1]
    P = w_branch_b.shape[1]
    assert x.shape[1] % MOBA_BLOCK == 0 and A % (2 * HEAD_DIM) == 0
    assert w_down.shape[1] % FF_CHUNK == 0 and 2 * POOL_WINDOWS[-1] <= POOL_HALO and POOL_WINDOWS == (2, 4, 8, 16) and CONV_WIDTH - 1 <= CONV_HALO
    bf = jnp.bfloat16
    w_in_first = w_in[:1].astype(bf)
    wvt = jnp.swapaxes(w_in[:, :, 2 * A:3 * A], 1, 2).astype(bf)
    early = (w_branch_a, w_pool, w_branch_b, w_out) + ((w_in[1:],) if depth > 1 else ())
    late = (w_up, w_down)
    flat = lambda ws: tuple(w.reshape(-1, w.shape[-1]) for w in ws)
    gf = norm_final_g.reshape(1, D)
    for layer in range(depth):
        w_in_l, w_layer = (w_in_first, 0) if layer == 0 else (w_in_rest, layer - 1)
        q, k, vt, u, *cast = _in_proj(x, norm_mix_g, w_in_l, w_layer, wvt, layer, A, P,
                                      casts=flat(early) if layer == 0 else ())
        if layer == 0:
            wa, wp, wb, wo, *rest = (c.reshape(w.shape) for c, w in zip(cast, early))
            w_in_rest = rest[0] if rest else None
        a_lo, a_hi = _moba(q, k, vt)
        x, *cast = _merge(x, a_lo, a_hi, u, norm_mix_g, w_in_l, w_layer, wa, wp, pool_scale, wb, wo, layer,
                          casts=flat(late) if layer == 0 else ())
        if layer == 0:
            wup, wdn = (c.reshape(w.shape) for c, w in zip(cast, late))
        x = _ffn(x, norm_ffn_g, wup, conv_w, conv_b, wdn, gf, layer, layer == depth - 1)
    return x
```

```python
import functools

import jax
import jax.numpy as jnp
from jax import lax
from jax.experimental import pallas as pl
from jax.experimental.pallas import tpu as pltpu

HEAD_DIM = 64
MOBA_BLOCK = 256
MOBA_TOPK = 3
POOL_WINDOWS = (2, 4, 8, 16)
CONV_WIDTH = 3
RMS_EPS = 1e-6
NEG_INF = -1e30
LOG2_E = 1.4426950408889634

ROW_TILE = 1024
FFN_ROW_TILE = 512
SUB_TILE = 256
POOL_HALO = 32
ONES_ROWS = 16
CONV_HALO = 8
FF_CHUNK = 256
V7X_VMEM_LIMIT = 56 * 1024 * 1024

_NT = (((1,), (1,)), ((), ()))


def _layer_operand(block, layer, index=None):
    index = (0,) * len(block) if index is None else index
    return pl.BlockSpec((None,) + tuple(block), lambda b, s: (layer,) + tuple(index),
                        pipeline_mode=pl.Buffered(1))


def _whole(shape):
    return pl.BlockSpec(shape, lambda b, s: (0,) * len(shape), pipeline_mode=pl.Buffered(1))


def _cast_specs(arrays, grid):
    n_steps = grid[0] * grid[1]
    specs, shapes = [], []
    for arr in arrays:
        rows, cols = arr.shape
        assert rows % (16 * n_steps) == 0, "bf16 row blocks are 16-row tiles"
        specs.append(pl.BlockSpec((rows // n_steps, cols), lambda b, s: (b * grid[1] + s, 0)))
        shapes.append(jax.ShapeDtypeStruct(arr.shape, jnp.bfloat16))
    return specs, shapes


def _cast_blocks(in_refs, out_refs):
    for src, dst in zip(in_refs, out_refs, strict=True):
        dst[...] = src[...].astype(dst.dtype)


def _rms_norm(x, g):
    y = x * lax.rsqrt(jnp.mean(x * x, axis=-1, keepdims=True) + RMS_EPS)
    return y * g


def _in_proj_kernel(x_ref, g_ref, wq_ref, wk_ref, wvt_ref, wu_ref, *refs, scale, layer, n_cast):
    cast_in, (q_ref, k_ref, vt_ref, u_ref), cast_out = refs[:n_cast], refs[n_cast:n_cast + 4], refs[n_cast + 4:]
    _cast_blocks(cast_in, cast_out)
    for j in range(vt_ref.shape[1]):
        rows = slice(j * MOBA_BLOCK, (j + 1) * MOBA_BLOCK)
        h = _rms_norm(x_ref[0, rows, :], g_ref[layer:layer + 1, :]).astype(jnp.bfloat16)
        q_ref[0, rows, :] = (jnp.dot(h, wq_ref[...], preferred_element_type=jnp.float32) * scale).astype(q_ref.dtype)
        k_ref[0, rows, :] = jnp.dot(h, wk_ref[...], preferred_element_type=jnp.float32).astype(k_ref.dtype)
        u_ref[0, rows, :] = jnp.dot(h, wu_ref[...], preferred_element_type=jnp.float32).astype(u_ref.dtype)
        vt_ref[0, j] = lax.dot_general(wvt_ref[...], h, _NT,
                                       preferred_element_type=jnp.float32).astype(vt_ref.dtype)


def _in_proj(x, g, w_in, wvt, layer, A, P, casts=()):
    B, S, D = x.shape
    assert P == A, "u is addressed as column block 3 of width A"
    T = min(ROW_TILE, S)
    assert S % T == 0 and T % SUB_TILE == 0
    nb_t = T // MOBA_BLOCK
    nb = S // MOBA_BLOCK
    grid = (B, S // T)
    cast_specs, cast_shapes = _cast_specs(casts, grid)
    return pl.pallas_call(
        functools.partial(_in_proj_kernel, scale=HEAD_DIM ** -0.5 * LOG2_E, layer=layer, n_cast=len(casts)),
        grid=grid,
        in_specs=[
            pl.BlockSpec((1, T, D), lambda b, s: (b, s, 0)),
            _whole(g.shape),
            _layer_operand((D, A), layer, (0, 0)),
            _layer_operand((D, A), layer, (0, 1)),
            _layer_operand((A, D), layer),
            _layer_operand((D, P), layer, (0, 3)),
            *cast_specs,
        ],
        out_specs=[
            pl.BlockSpec((1, T, A), lambda b, s: (b, s, 0)),
            pl.BlockSpec((1, T, A), lambda b, s: (b, s, 0)),
            pl.BlockSpec((1, nb_t, A, MOBA_BLOCK), lambda b, s: (b, s, 0, 0)),
            pl.BlockSpec((1, T, P), lambda b, s: (b, s, 0)),
            *cast_specs,
        ],
        out_shape=[
            jax.ShapeDtypeStruct((B, S, A), jnp.bfloat16),
            jax.ShapeDtypeStruct((B, S, A), jnp.bfloat16),
            jax.ShapeDtypeStruct((B, nb, A, MOBA_BLOCK), jnp.bfloat16),
            jax.ShapeDtypeStruct((B, S, P), jnp.bfloat16),
            *cast_shapes,
        ],
        compiler_params=pltpu.CompilerParams(
            dimension_semantics=("parallel", "parallel"), vmem_limit_bytes=V7X_VMEM_LIMIT),
        name="in_proj",
    )(x, g, w_in, w_in, wvt, w_in, *casts)


def _moba_kernel(qa_ref, qb_ref, k_ref, vt_ref, oa_ref, ob_ref, kmean_ref, qza_ref, qzb_ref,
                 s0_ref, mb0_ref, s1_ref, mb1_ref, ota_ref, otb_ref, *, n_heads, n_blocks):
    L = MOBA_BLOCK
    G = 2 * HEAD_DIM
    pair = pl.program_id(1)

    @pl.when(pair == 0)
    def _():
        row = lax.broadcasted_iota(jnp.int32, (n_blocks, n_blocks * L), 0)
        col = lax.broadcasted_iota(jnp.int32, (n_blocks, n_blocks * L), 1)
        ind = jnp.where((col >= row * L) & (col < (row + 1) * L), 1.0 / L, 0.0).astype(jnp.bfloat16)
        k_all = k_ref[0].reshape(n_blocks * L, k_ref.shape[-1])
        kmean_ref[...] = jnp.dot(ind, k_all, preferred_element_type=jnp.float32)

    blk = lax.broadcasted_iota(jnp.int32, (n_blocks, L), 0)
    key_pos = lax.broadcasted_iota(jnp.int32, (L, L), 0)
    qry_pos = lax.broadcasted_iota(jnp.int32, (L, L), 1)
    lane = lax.broadcasted_iota(jnp.int32, (L, G), 1)
    ones_rows = jnp.ones((ONES_ROWS, L), jnp.bfloat16)
    groups = [slice((h // 2) * G, (h // 2 + 1) * G) for h in range(n_heads)]
    slots = ((s0_ref, mb0_ref), (s1_ref, mb1_ref))

    def pv(b, h, p):
        lhs = jnp.concatenate([vt_ref[0, b, h * HEAD_DIM:(h + 1) * HEAD_DIM, :], ones_rows], axis=0)
        return jnp.dot(lhs, p.astype(jnp.bfloat16), preferred_element_type=jnp.float32)

    def stage_a(qz_ref, b, h, slot, causal):
        s_ref, mb_ref = slot
        s = lax.dot_general(k_ref[0, b, :, groups[h]], qz_ref[h], _NT, preferred_element_type=jnp.float32)
        if causal:
            s = jnp.where(key_pos <= qry_pos, s, NEG_INF)
        s_ref[h] = s
        mb_ref[h:h + 1, :] = jnp.max(s, axis=0, keepdims=True)

    def stage_b(b, h, slot, state, on):
        s_ref, mb_ref = slot
        mb = mb_ref[h:h + 1, :]
        if state is None:
            return mb, pv(b, h, jnp.exp2(s_ref[h] - mb))
        m, acc = state
        if on is None:
            m_new = jnp.maximum(m, mb)
            shift = m_new
        else:
            m_new = jnp.where(on, jnp.maximum(m, mb), m)
            shift = jnp.where(on, m_new, -NEG_INF)
        return m_new, jnp.exp2(m - m_new) * acc + pv(b, h, jnp.exp2(s_ref[h] - shift))

    def prepare(ti, q_ref, qz_ref):
        sel = []
        for h in range(n_heads):
            in_head = (lane >= (h % 2) * HEAD_DIM) & (lane < (h % 2 + 1) * HEAD_DIM)
            qz_ref[h] = jnp.where(in_head, q_ref[0, :, groups[h]], jnp.zeros((L, G), q_ref.dtype))
            if ti <= MOBA_TOPK:
                sel.append(None)
                continue
            km = kmean_ref[:, groups[h]]
            km_hi = km.astype(jnp.bfloat16)
            km_lo = (km - km_hi.astype(jnp.float32)).astype(jnp.bfloat16)
            g2 = lax.dot_general(jnp.concatenate([km_hi, km_lo], axis=0), qz_ref[h], _NT,
                                 preferred_element_type=jnp.float32)
            gate = g2[:n_blocks] + g2[n_blocks:]
            rank = jnp.zeros((n_blocks, L), jnp.float32)
            for j in range(ti):
                gj = gate[j:j + 1, :]
                rank = rank + jnp.where((gj > gate) | ((gj == gate) & (blk > j)), 1.0, 0.0)
            sel.append(jnp.where(rank < MOBA_TOPK, 1.0, 0.0))
        return sel

    def finalize(state, ot_ref, o_ref):
        for h in range(n_heads):
            acc = state[h][1]
            inv_l = 1.0 / acc[HEAD_DIM:HEAD_DIM + 1, :]
            ot_ref[h * HEAD_DIM:(h + 1) * HEAD_DIM, :] = (acc[0:HEAD_DIM, :] * inv_l).astype(ot_ref.dtype)
        eye = jnp.where(key_pos == qry_pos, 1.0, 0.0).astype(jnp.bfloat16)
        o_ref[0] = lax.dot_general(eye, ot_ref[...], _NT, preferred_element_type=jnp.float32).astype(o_ref.dtype)

    def tile_pair(tp):
        tiles = ((tp, qa_ref, qza_ref, ota_ref, oa_ref), (n_blocks - 1 - tp, qb_ref, qzb_ref, otb_ref, ob_ref))
        sels = [prepare(ti, q_ref, qz_ref) for ti, q_ref, qz_ref, _, _ in tiles]
        visits = [(w, b, n == 0, n == tiles[w][0]) for w in range(2)
                  for n, b in enumerate([tiles[w][0]] + list(range(tiles[w][0])))]
        states = [[None] * n_heads, [None] * n_heads]
        for h in range(n_heads):
            stage_a(tiles[0][2], visits[0][1], h, slots[0], causal=True)
        for n, (w, b, first, last) in enumerate(visits):
            for h in range(n_heads):
                if n + 1 < len(visits):
                    w2, b2, first2, _ = visits[n + 1]
                    stage_a(tiles[w2][2], b2, h, slots[(n + 1) % 2], causal=first2)
                on = None if (first or sels[w][h] is None) else sels[w][h][b:b + 1, :] > 0.0
                states[w][h] = stage_b(b, h, slots[n % 2], states[w][h], on)
            if last:
                finalize(states[w], tiles[w][3], tiles[w][4])

    for tp in range(n_blocks // 2):
        pl.when(pair == tp)(functools.partial(tile_pair, tp))


def _moba(q, k, vt):
    B, S, A = q.shape
    L = MOBA_BLOCK
    nb = S // L
    H = A // HEAD_DIM
    assert nb % 2 == 0
    k4 = k.reshape(B, nb, L, A)
    half = jax.ShapeDtypeStruct((B, S // 2, A), jnp.bfloat16)
    return pl.pallas_call(
        functools.partial(_moba_kernel, n_heads=H, n_blocks=nb),
        grid=(B, nb // 2),
        in_specs=[
            pl.BlockSpec((1, L, A), lambda b, j: (b, j, 0)),
            pl.BlockSpec((1, L, A), lambda b, j: (b, nb - 1 - j, 0)),
            pl.BlockSpec((1, nb, L, A), lambda b, j: (b, 0, 0, 0)),
            pl.BlockSpec((1, nb, A, L), lambda b, j: (b, 0, 0, 0)),
        ],
        out_specs=[
            pl.BlockSpec((1, L, A), lambda b, j: (b, j, 0)),
            pl.BlockSpec((1, L, A), lambda b, j: (b, nb // 2 - 1 - j, 0)),
        ],
        out_shape=[half, half],
        scratch_shapes=[
            pltpu.VMEM((nb, A), jnp.float32),
            pltpu.VMEM((H, L, 2 * HEAD_DIM), jnp.bfloat16),
            pltpu.VMEM((H, L, 2 * HEAD_DIM), jnp.bfloat16),
            pltpu.VMEM((H, L, L), jnp.float32),
            pltpu.VMEM((H, L), jnp.float32),
            pltpu.VMEM((H, L, L), jnp.float32),
            pltpu.VMEM((H, L), jnp.float32),
            pltpu.VMEM((A, L), jnp.bfloat16),
            pltpu.VMEM((A, L), jnp.bfloat16),
        ],
        compiler_params=pltpu.CompilerParams(
            dimension_semantics=("parallel", "arbitrary"), vmem_limit_bytes=V7X_VMEM_LIMIT),
        name="moba",
    )(q, q, k4, vt)


def _sigmoid(x):
    return 0.5 * jnp.tanh(0.5 * x) + 0.5


def _merge_kernel(x_ref, a_lo_ref, a_hi_ref, u_ref, g_ref, wg_ref, wa_ref, wp_ref, ps_ref, wb_ref, wo_ref,
                  *refs, layer, n_lo, n_cast):
    cast_in, o_ref, cast_out = refs[:n_cast], refs[n_cast], refs[n_cast + 1:2 * n_cast + 1]
    uext_ref, w2_ref, w4_ref, w8_ref = refs[2 * n_cast + 1:]
    _cast_blocks(cast_in, cast_out)
    T = x_ref.shape[1]
    D = x_ref.shape[2]
    GW = wp_ref.shape[1]
    R = SUB_TILE
    HALO = POOL_HALO
    s_idx = pl.program_id(1)

    @pl.when(s_idx == 0)
    def _():
        uext_ref[0:HALO, :] = jnp.zeros((HALO, uext_ref.shape[1]), jnp.float32)

    def branch_in(r0):
        a = jnp.where(s_idx < n_lo, a_lo_ref[0, r0:r0 + R, :], a_hi_ref[0, r0:r0 + R, :])
        y_a = jnp.dot(a, wa_ref[...], preferred_element_type=jnp.float32)
        h = _rms_norm(x_ref[0, r0:r0 + R, :], g_ref[layer:layer + 1, :]).astype(jnp.bfloat16)
        return y_a, jnp.dot(h, wg_ref[...], preferred_element_type=jnp.float32)

    def window_sums():
        n = HALO + T
        uext_ref[HALO:n, :] = u_ref[0].astype(jnp.float32)
        w2_ref[8:n, :] = uext_ref[8:n, :] + uext_ref[7:n - 1, :]
        w4_ref[16:n, GW:] = w2_ref[16:n, GW:] + w2_ref[14:n - 2, GW:]
        w8_ref[24:n, 2 * GW:] = w4_ref[24:n, 2 * GW:] + w4_ref[20:n - 4, 2 * GW:]

    def branch_out(r0, y_a, g_pre):
        e0 = HALO + r0
        wsums = (w2_ref[e0:e0 + R, 0:GW], w4_ref[e0:e0 + R, GW:2 * GW], w8_ref[e0:e0 + R, 2 * GW:3 * GW],
                 w8_ref[e0:e0 + R, 3 * GW:] + w8_ref[e0 - 8:e0 - 8 + R, 3 * GW:])
        t_pos = s_idx * T + r0 + lax.broadcasted_iota(jnp.int32, (R, GW), 0)
        ys = []
        for gi, win in enumerate(POOL_WINDOWS):
            count = jnp.minimum(t_pos + 1, win).astype(jnp.float32)
            mixed = (wsums[gi] / count - uext_ref[e0:e0 + R, gi * GW:(gi + 1) * GW]).astype(jnp.bfloat16)
            ys.append(jnp.dot(mixed, wp_ref[gi], preferred_element_type=jnp.float32))
        pooled = (jnp.concatenate(ys, axis=1) * ps_ref[layer:layer + 1, :]).astype(jnp.bfloat16)
        y_b = jnp.dot(pooled, wb_ref[...], preferred_element_type=jnp.float32)
        gates = _sigmoid(g_pre)
        merged = (gates[:, :D] * y_a + gates[:, D:] * y_b).astype(jnp.bfloat16)
        o_ref[0, r0:r0 + R, :] = x_ref[0, r0:r0 + R, :] + jnp.dot(merged, wo_ref[...],
                                                                    preferred_element_type=jnp.float32)

    n_sub = T // R
    pending = branch_in(0)
    window_sums()
    for j in range(n_sub):
        nxt = branch_in((j + 1) * R) if j + 1 < n_sub else None
        branch_out(j * R, *pending)
        pending = nxt
    uext_ref[0:HALO, :] = uext_ref[T:T + HALO, :]


def _merge(x, a_lo, a_hi, u, g, w_in, wa, wp, ps, wb, wo, layer, casts=()):
    B, S, D = x.shape
    A = a_lo.shape[2]
    P = u.shape[2]
    assert 3 * A + P == 2 * D, "the branch gates are addressed as column block 1 of width 2D"
    T = min(ROW_TILE, S)
    assert S % T == 0 and T % SUB_TILE == 0
    row = lambda b, s: (b, s, 0)
    n_lo = S // T // 2
    assert S % (2 * T) == 0
    grid = (B, S // T)
    cast_specs, cast_shapes = _cast_specs(casts, grid)
    return pl.pallas_call(
        functools.partial(_merge_kernel, layer=layer, n_lo=n_lo, n_cast=len(casts)),
        grid=grid,
        in_specs=[
            pl.BlockSpec((1, T, D), row),
            pl.BlockSpec((1, T, A), lambda b, s: (b, jnp.minimum(s, n_lo - 1), 0)),
            pl.BlockSpec((1, T, A), lambda b, s: (b, jnp.maximum(s - n_lo, 0), 0)),
            pl.BlockSpec((1, T, P), row),
            _whole(g.shape),
            _layer_operand((D, 2 * D), layer, (0, 1)),
            _layer_operand(wa.shape[1:], layer),
            _layer_operand(wp.shape[1:], layer),
            _whole(ps.shape),
            _layer_operand(wb.shape[1:], layer),
            _layer_operand(wo.shape[1:], layer),
            *cast_specs,
        ],
        out_specs=[pl.BlockSpec((1, T, D), row), *cast_specs],
        out_shape=[jax.ShapeDtypeStruct((B, S, D), x.dtype), *cast_shapes],
        scratch_shapes=[pltpu.VMEM((POOL_HALO + T, P), jnp.float32)] * 4,
        compiler_params=pltpu.CompilerParams(
            dimension_semantics=("parallel", "arbitrary"), vmem_limit_bytes=V7X_VMEM_LIMIT),
        name="merge",
    )(x, a_lo, a_hi, u, g, w_in, wa, wp, ps, wb, wo, *casts)


def _ffn_kernel(x_ref, g_ref, wup_ref, cw_ref, cb_ref, wdn_ref, gf_ref, o_ref,
                ext_ref, carry_ref, act_ref, *, layer, final_norm):
    T = x_ref.shape[1]
    F = wdn_ref.shape[0]
    FC = FF_CHUNK
    H = CONV_HALO
    s_idx = pl.program_id(1)

    @pl.when(s_idx == 0)
    def _():
        carry_ref[...] = jnp.zeros(carry_ref.shape, jnp.float32)

    h = _rms_norm(x_ref[0], g_ref[layer:layer + 1, :]).astype(jnp.bfloat16)

    def conv_cols(c0, scale):
        a = jnp.dot(h, wup_ref[:, c0:c0 + FC], preferred_element_type=jnp.float32)
        ext_ref[0:H, :] = carry_ref[:, c0:c0 + FC]
        ext_ref[H:H + T, :] = a
        carry_ref[:, c0:c0 + FC] = a[T - H:, :]
        w = cw_ref[:, c0:c0 + FC] * scale
        return (cb_ref[layer:layer + 1, c0:c0 + FC] * scale + w[2:3, :] * a
                + w[1:2, :] * ext_ref[H - 1:H - 1 + T, :]
                + w[0:1, :] * ext_ref[H - 2:H - 2 + T, :])

    for c in range(F // FC):
        half_gate = conv_cols(c * FC, 0.5)
        val = conv_cols(F + c * FC, 1.0)
        act_ref[:, c * FC:(c + 1) * FC] = (half_gate * (jnp.tanh(half_gate) + 1.0) * val).astype(act_ref.dtype)

    y = x_ref[0] + jnp.dot(act_ref[...], wdn_ref[...], preferred_element_type=jnp.float32)
    if final_norm:
        y = _rms_norm(y, gf_ref[...])
    o_ref[0] = y


def _ffn(x, g, wup, cw, cb, wdn, gf, layer, final_norm):
    B, S, D = x.shape
    F = wdn.shape[1]
    T = min(FFN_ROW_TILE, S)
    assert S % T == 0
    row = lambda b, s: (b, s, 0)
    return pl.pallas_call(
        functools.partial(_ffn_kernel, layer=layer, final_norm=final_norm),
        grid=(B, S // T),
        in_specs=[
            pl.BlockSpec((1, T, D), row),
            _whole(g.shape),
            _layer_operand(wup.shape[1:], layer),
            _layer_operand(cw.shape[1:], layer),
            _whole(cb.shape),
            _layer_operand(wdn.shape[1:], layer),
            _whole(gf.shape),
        ],
        out_specs=pl.BlockSpec((1, T, D), row),
        out_shape=jax.ShapeDtypeStruct((B, S, D), x.dtype),
        scratch_shapes=[
            pltpu.VMEM((CONV_HALO + T, FF_CHUNK), jnp.float32),
            pltpu.VMEM((CONV_HALO, 2 * F), jnp.float32),
            pltpu.VMEM((T, F), jnp.bfloat16),
        ],
        compiler_params=pltpu.CompilerParams(
            dimension_semantics=("parallel", "arbitrary"), vmem_limit_bytes=V7X_VMEM_LIMIT),
        name="ffn",
    )(x, g, wup, cw, cb, wdn, gf)


def kernel(x, norm_mix_g, w_in, w_pool, pool_scale, w_branch_a, w_branch_b, w_out, norm_ffn_g, w_up, conv_w, conv_b, w_down, norm_final_g):
    depth = w_in.shape[0]
    D = x.shape[-1]
    A = w_branch_a.shape[1]
    P = w_branch_b.shape[1]
    assert x.shape[1] % MOBA_BLOCK == 0 and A % (2 * HEAD_DIM) == 0
    assert w_down.shape[1] % FF_CHUNK == 0 and 2 * POOL_WINDOWS[-1] <= POOL_HALO and POOL_WINDOWS == (2, 4, 8, 16) and CONV_WIDTH - 1 <= CONV_HALO
    bf = jnp.bfloat16
    w_in_bf = w_in.astype(bf)
    wvt = jnp.swapaxes(w_in[:, :, 2 * A:3 * A], 1, 2).astype(bf)
    early = (w_branch_a, w_pool, w_branch_b, w_out)
    late = (w_up, w_down)
    flat = lambda ws: tuple(w.reshape(-1, w.shape[-1]) for w in ws)
    gf = norm_final_g.reshape(1, D)
    for layer in range(depth):
        q, k, vt, u, *cast = _in_proj(x, norm_mix_g, w_in_bf, wvt, layer, A, P,
                                      casts=flat(early) if layer == 0 else ())
        if layer == 0:
            wa, wp, wb, wo = (c.reshape(w.shape) for c, w in zip(cast, early))
        a_lo, a_hi = _moba(q, k, vt)
        x, *cast = _merge(x, a_lo, a_hi, u, norm_mix_g, w_in_bf, wa, wp, pool_scale, wb, wo, layer,
                          casts=flat(late) if layer == 0 else ())
        if layer == 0:
            wup, wdn = (c.reshape(w.shape) for c, w in zip(cast, late))
        x = _ffn(x, norm_ffn_g, wup, conv_w, conv_b, wdn, gf, layer, layer == depth - 1)
    return x
```

```python
import functools

import jax
import jax.numpy as jnp
from jax import lax
from jax.experimental import pallas as pl
from jax.experimental.pallas import tpu as pltpu

HEAD_DIM = 64
MOBA_BLOCK = 256
MOBA_TOPK = 3
POOL_WINDOWS = (2, 4, 8, 16)
CONV_WIDTH = 3
RMS_EPS = 1e-6
NEG_INF = -1e30
LOG2_E = 1.4426950408889634

ROW_TILE = 1024
FFN_ROW_TILE = 512
SUB_TILE = 256
POOL_HALO = 32
ONES_ROWS = 16
CONV_HALO = 8
FF_CHUNK = 256
V7X_VMEM_LIMIT = 56 * 1024 * 1024

_NT = (((1,), (1,)), ((), ()))


def _layer_operand(block, layer, index=None):
    index = (0,) * len(block) if index is None else index
    return pl.BlockSpec((None,) + tuple(block), lambda b, s: (layer,) + tuple(index),
                        pipeline_mode=pl.Buffered(1))


def _whole(shape):
    return pl.BlockSpec(shape, lambda b, s: (0,) * len(shape), pipeline_mode=pl.Buffered(1))


def _cast_specs(arrays, grid):
    n_steps = grid[0] * grid[1]
    in_specs, out_specs, shapes = [], [], []
    for arr, cols, col_block in arrays:
        rows = arr.shape[0]
        assert rows % (16 * n_steps) == 0, "bf16 row blocks are 16-row tiles"
        in_specs.append(pl.BlockSpec((rows // n_steps, cols), lambda b, s, j=col_block: (b * grid[1] + s, j)))
        out_specs.append(pl.BlockSpec((rows // n_steps, cols), lambda b, s: (b * grid[1] + s, 0)))
        shapes.append(jax.ShapeDtypeStruct((rows, cols), jnp.bfloat16))
    return in_specs, out_specs, shapes


def _cast_blocks(in_refs, out_refs):
    for src, dst in zip(in_refs, out_refs, strict=True):
        dst[...] = src[...].astype(dst.dtype)


def _rms_norm(x, g):
    y = x * lax.rsqrt(jnp.mean(x * x, axis=-1, keepdims=True) + RMS_EPS)
    return y * g


def _in_proj_kernel(x_ref, g_ref, wq32_ref, wk32_ref, wvt_ref, wu32_ref, *refs, scale, layer, n_cast):
    cast_in, (q_ref, k_ref, vt_ref, u_ref) = refs[:n_cast], refs[n_cast:n_cast + 4]
    cast_out, (wq_ref, wk_ref, wu_ref) = refs[n_cast + 4:2 * n_cast + 4], refs[2 * n_cast + 4:]
    _cast_blocks(cast_in, cast_out)

    @pl.when((pl.program_id(0) == 0) & (pl.program_id(1) == 0))
    def _():
        _cast_blocks((wq32_ref, wk32_ref, wu32_ref), (wq_ref, wk_ref, wu_ref))

    for j in range(vt_ref.shape[1]):
        rows = slice(j * MOBA_BLOCK, (j + 1) * MOBA_BLOCK)
        h = _rms_norm(x_ref[0, rows, :], g_ref[layer:layer + 1, :]).astype(jnp.bfloat16)
        q_ref[0, rows, :] = (jnp.dot(h, wq_ref[...], preferred_element_type=jnp.float32) * scale).astype(q_ref.dtype)
        k_ref[0, rows, :] = jnp.dot(h, wk_ref[...], preferred_element_type=jnp.float32).astype(k_ref.dtype)
        u_ref[0, rows, :] = jnp.dot(h, wu_ref[...], preferred_element_type=jnp.float32).astype(u_ref.dtype)
        vt_ref[0, j] = lax.dot_general(wvt_ref[...], h, _NT,
                                       preferred_element_type=jnp.float32).astype(vt_ref.dtype)


def _in_proj(x, g, w_in, wvt, layer, A, P, casts=()):
    B, S, D = x.shape
    assert P == A, "u is addressed as column block 3 of width A"
    T = min(ROW_TILE, S)
    assert S % T == 0 and T % SUB_TILE == 0
    nb_t = T // MOBA_BLOCK
    nb = S // MOBA_BLOCK
    grid = (B, S // T)
    cast_in_specs, cast_out_specs, cast_shapes = _cast_specs(casts, grid)
    return pl.pallas_call(
        functools.partial(_in_proj_kernel, scale=HEAD_DIM ** -0.5 * LOG2_E, layer=layer, n_cast=len(casts)),
        grid=grid,
        in_specs=[
            pl.BlockSpec((1, T, D), lambda b, s: (b, s, 0)),
            _whole(g.shape),
            _layer_operand((D, A), layer, (0, 0)),
            _layer_operand((D, A), layer, (0, 1)),
            _layer_operand((A, D), layer),
            _layer_operand((D, P), layer, (0, 3)),
            *cast_in_specs,
        ],
        out_specs=[
            pl.BlockSpec((1, T, A), lambda b, s: (b, s, 0)),
            pl.BlockSpec((1, T, A), lambda b, s: (b, s, 0)),
            pl.BlockSpec((1, nb_t, A, MOBA_BLOCK), lambda b, s: (b, s, 0, 0)),
            pl.BlockSpec((1, T, P), lambda b, s: (b, s, 0)),
            *cast_out_specs,
        ],
        out_shape=[
            jax.ShapeDtypeStruct((B, S, A), jnp.bfloat16),
            jax.ShapeDtypeStruct((B, S, A), jnp.bfloat16),
            jax.ShapeDtypeStruct((B, nb, A, MOBA_BLOCK), jnp.bfloat16),
            jax.ShapeDtypeStruct((B, S, P), jnp.bfloat16),
            *cast_shapes,
        ],
        scratch_shapes=[pltpu.VMEM((D, A), jnp.bfloat16), pltpu.VMEM((D, A), jnp.bfloat16),
                        pltpu.VMEM((D, P), jnp.bfloat16)],
        compiler_params=pltpu.CompilerParams(
            dimension_semantics=("arbitrary", "arbitrary"), vmem_limit_bytes=V7X_VMEM_LIMIT),
        name="in_proj",
    )(x, g, w_in, w_in, wvt, w_in, *(c[0] for c in casts))


def _moba_kernel(qa_ref, qb_ref, k_ref, vt_ref, oa_ref, ob_ref, kmean_ref, qza_ref, qzb_ref,
                 s0_ref, mb0_ref, s1_ref, mb1_ref, ota_ref, otb_ref, *, n_heads, n_blocks):
    L = MOBA_BLOCK
    G = 2 * HEAD_DIM
    pair = pl.program_id(1)

    @pl.when(pair == 0)
    def _():
        row = lax.broadcasted_iota(jnp.int32, (n_blocks, n_blocks * L), 0)
        col = lax.broadcasted_iota(jnp.int32, (n_blocks, n_blocks * L), 1)
        ind = jnp.where((col >= row * L) & (col < (row + 1) * L), 1.0 / L, 0.0).astype(jnp.bfloat16)
        k_all = k_ref[0].reshape(n_blocks * L, k_ref.shape[-1])
        kmean_ref[...] = jnp.dot(ind, k_all, preferred_element_type=jnp.float32)

    blk = lax.broadcasted_iota(jnp.int32, (n_blocks, L), 0)
    key_pos = lax.broadcasted_iota(jnp.int32, (L, L), 0)
    qry_pos = lax.broadcasted_iota(jnp.int32, (L, L), 1)
    lane = lax.broadcasted_iota(jnp.int32, (L, G), 1)
    ones_rows = jnp.ones((ONES_ROWS, L), jnp.bfloat16)
    groups = [slice((h // 2) * G, (h // 2 + 1) * G) for h in range(n_heads)]
    slots = ((s0_ref, mb0_ref), (s1_ref, mb1_ref))

    def pv(b, h, p):
        lhs = jnp.concatenate([vt_ref[0, b, h * HEAD_DIM:(h + 1) * HEAD_DIM, :], ones_rows], axis=0)
        return jnp.dot(lhs, p.astype(jnp.bfloat16), preferred_element_type=jnp.float32)

    def stage_a(qz_ref, b, h, slot, causal):
        s_ref, mb_ref = slot
        s = lax.dot_general(k_ref[0, b, :, groups[h]], qz_ref[h], _NT, preferred_element_type=jnp.float32)
        if causal:
            s = jnp.where(key_pos <= qry_pos, s, NEG_INF)
        s_ref[h] = s
        mb_ref[h:h + 1, :] = jnp.max(s, axis=0, keepdims=True)

    def stage_b(b, h, slot, state, on):
        s_ref, mb_ref = slot
        mb = mb_ref[h:h + 1, :]
        if state is None:
            return mb, pv(b, h, jnp.exp2(s_ref[h] - mb))
        m, acc = state
        if on is None:
            m_new = jnp.maximum(m, mb)
            shift = m_new
        else:
            m_new = jnp.where(on, jnp.maximum(m, mb), m)
            shift = jnp.where(on, m_new, -NEG_INF)
        return m_new, jnp.exp2(m - m_new) * acc + pv(b, h, jnp.exp2(s_ref[h] - shift))

    def prepare(ti, q_ref, qz_ref):
        sel = []
        for h in range(n_heads):
            in_head = (lane >= (h % 2) * HEAD_DIM) & (lane < (h % 2 + 1) * HEAD_DIM)
            qz_ref[h] = jnp.where(in_head, q_ref[0, :, groups[h]], jnp.zeros((L, G), q_ref.dtype))
            if ti <= MOBA_TOPK:
                sel.append(None)
                continue
            km = kmean_ref[:, groups[h]]
            km_hi = km.astype(jnp.bfloat16)
            km_lo = (km - km_hi.astype(jnp.float32)).astype(jnp.bfloat16)
            g2 = lax.dot_general(jnp.concatenate([km_hi, km_lo], axis=0), qz_ref[h], _NT,
                                 preferred_element_type=jnp.float32)
            gate = g2[:n_blocks] + g2[n_blocks:]
            rank = jnp.zeros((n_blocks, L), jnp.float32)
            for j in range(ti):
                gj = gate[j:j + 1, :]
                rank = rank + jnp.where((gj > gate) | ((gj == gate) & (blk > j)), 1.0, 0.0)
            sel.append(jnp.where(rank < MOBA_TOPK, 1.0, 0.0))
        return sel

    def finalize(state, ot_ref, o_ref):
        for h in range(n_heads):
            acc = state[h][1]
            inv_l = 1.0 / acc[HEAD_DIM:HEAD_DIM + 1, :]
            ot_ref[h * HEAD_DIM:(h + 1) * HEAD_DIM, :] = (acc[0:HEAD_DIM, :] * inv_l).astype(ot_ref.dtype)
        eye = jnp.where(key_pos == qry_pos, 1.0, 0.0).astype(jnp.bfloat16)
        o_ref[0] = lax.dot_general(eye, ot_ref[...], _NT, preferred_element_type=jnp.float32).astype(o_ref.dtype)

    def tile_pair(tp):
        tiles = ((tp, qa_ref, qza_ref, ota_ref, oa_ref), (n_blocks - 1 - tp, qb_ref, qzb_ref, otb_ref, ob_ref))
        sels = [prepare(ti, q_ref, qz_ref) for ti, q_ref, qz_ref, _, _ in tiles]
        visits = [(w, b, n == 0, n == tiles[w][0]) for w in range(2)
                  for n, b in enumerate([tiles[w][0]] + list(range(tiles[w][0])))]
        states = [[None] * n_heads, [None] * n_heads]
        for h in range(n_heads):
            stage_a(tiles[0][2], visits[0][1], h, slots[0], causal=True)
        for n, (w, b, first, last) in enumerate(visits):
            for h in range(n_heads):
                if n + 1 < len(visits):
                    w2, b2, first2, _ = visits[n + 1]
                    stage_a(tiles[w2][2], b2, h, slots[(n + 1) % 2], causal=first2)
                on = None if (first or sels[w][h] is None) else sels[w][h][b:b + 1, :] > 0.0
                states[w][h] = stage_b(b, h, slots[n % 2], states[w][h], on)
            if last:
                finalize(states[w], tiles[w][3], tiles[w][4])

    for tp in range(n_blocks // 2):
        pl.when(pair == tp)(functools.partial(tile_pair, tp))


def _moba(q, k, vt):
    B, S, A = q.shape
    L = MOBA_BLOCK
    nb = S // L
    H = A // HEAD_DIM
    assert nb % 2 == 0
    k4 = k.reshape(B, nb, L, A)
    half = jax.ShapeDtypeStruct((B, S // 2, A), jnp.bfloat16)
    return pl.pallas_call(
        functools.partial(_moba_kernel, n_heads=H, n_blocks=nb),
        grid=(B, nb // 2),
        in_specs=[
            pl.BlockSpec((1, L, A), lambda b, j: (b, j, 0)),
            pl.BlockSpec((1, L, A), lambda b, j: (b, nb - 1 - j, 0)),
            pl.BlockSpec((1, nb, L, A), lambda b, j: (b, 0, 0, 0)),
            pl.BlockSpec((1, nb, A, L), lambda b, j: (b, 0, 0, 0)),
        ],
        out_specs=[
            pl.BlockSpec((1, L, A), lambda b, j: (b, j, 0)),
            pl.BlockSpec((1, L, A), lambda b, j: (b, nb // 2 - 1 - j, 0)),
        ],
        out_shape=[half, half],
        scratch_shapes=[
            pltpu.VMEM((nb, A), jnp.float32),
            pltpu.VMEM((H, L, 2 * HEAD_DIM), jnp.bfloat16),
            pltpu.VMEM((H, L, 2 * HEAD_DIM), jnp.bfloat16),
            pltpu.VMEM((H, L, L), jnp.float32),
            pltpu.VMEM((H, L), jnp.float32),
            pltpu.VMEM((H, L, L), jnp.float32),
            pltpu.VMEM((H, L), jnp.float32),
            pltpu.VMEM((A, L), jnp.bfloat16),
            pltpu.VMEM((A, L), jnp.bfloat16),
        ],
        compiler_params=pltpu.CompilerParams(
            dimension_semantics=("parallel", "arbitrary"), vmem_limit_bytes=V7X_VMEM_LIMIT),
        name="moba",
    )(q, q, k4, vt)


def _sigmoid(x):
    return 0.5 * jnp.tanh(0.5 * x) + 0.5


def _merge_kernel(x_ref, a_lo_ref, a_hi_ref, u_ref, g_ref, wg_ref, wa_ref, wp_ref, ps_ref, wb_ref, wo_ref,
                  *refs, layer, n_lo, n_cast):
    cast_in, o_ref, cast_out = refs[:n_cast], refs[n_cast], refs[n_cast + 1:2 * n_cast + 1]
    uext_ref, w2_ref, w4_ref, w8_ref = refs[2 * n_cast + 1:]
    _cast_blocks(cast_in, cast_out)
    T = x_ref.shape[1]
    D = x_ref.shape[2]
    GW = wp_ref.shape[1]
    R = SUB_TILE
    HALO = POOL_HALO
    s_idx = pl.program_id(1)

    @pl.when(s_idx == 0)
    def _():
        uext_ref[0:HALO, :] = jnp.zeros((HALO, uext_ref.shape[1]), jnp.float32)

    def branch_in(r0):
        a = jnp.where(s_idx < n_lo, a_lo_ref[0, r0:r0 + R, :], a_hi_ref[0, r0:r0 + R, :])
        y_a = jnp.dot(a, wa_ref[...], preferred_element_type=jnp.float32)
        h = _rms_norm(x_ref[0, r0:r0 + R, :], g_ref[layer:layer + 1, :]).astype(jnp.bfloat16)
        return y_a, jnp.dot(h, wg_ref[...], preferred_element_type=jnp.float32)

    def window_sums():
        n = HALO + T
        uext_ref[HALO:n, :] = u_ref[0].astype(jnp.float32)
        w2_ref[8:n, :] = uext_ref[8:n, :] + uext_ref[7:n - 1, :]
        w4_ref[16:n, GW:] = w2_ref[16:n, GW:] + w2_ref[14:n - 2, GW:]
        w8_ref[24:n, 2 * GW:] = w4_ref[24:n, 2 * GW:] + w4_ref[20:n - 4, 2 * GW:]

    def branch_out(r0, y_a, g_pre):
        e0 = HALO + r0
        wsums = (w2_ref[e0:e0 + R, 0:GW], w4_ref[e0:e0 + R, GW:2 * GW], w8_ref[e0:e0 + R, 2 * GW:3 * GW],
                 w8_ref[e0:e0 + R, 3 * GW:] + w8_ref[e0 - 8:e0 - 8 + R, 3 * GW:])
        t_pos = s_idx * T + r0 + lax.broadcasted_iota(jnp.int32, (R, GW), 0)
        ys = []
        for gi, win in enumerate(POOL_WINDOWS):
            count = jnp.minimum(t_pos + 1, win).astype(jnp.float32)
            mixed = (wsums[gi] / count - uext_ref[e0:e0 + R, gi * GW:(gi + 1) * GW]).astype(jnp.bfloat16)
            ys.append(jnp.dot(mixed, wp_ref[gi], preferred_element_type=jnp.float32))
        pooled = (jnp.concatenate(ys, axis=1) * ps_ref[layer:layer + 1, :]).astype(jnp.bfloat16)
        y_b = jnp.dot(pooled, wb_ref[...], preferred_element_type=jnp.float32)
        gates = _sigmoid(g_pre)
        merged = (gates[:, :D] * y_a + gates[:, D:] * y_b).astype(jnp.bfloat16)
        o_ref[0, r0:r0 + R, :] = x_ref[0, r0:r0 + R, :] + jnp.dot(merged, wo_ref[...],
                                                                    preferred_element_type=jnp.float32)

    n_sub = T // R
    pending = branch_in(0)
    window_sums()
    for j in range(n_sub):
        nxt = branch_in((j + 1) * R) if j + 1 < n_sub else None
        branch_out(j * R, *pending)
        pending = nxt
    uext_ref[0:HALO, :] = uext_ref[T:T + HALO, :]


def _merge(x, a_lo, a_hi, u, g, wg, wa, wp, ps, wb, wo, layer, casts=()):
    B, S, D = x.shape
    A = a_lo.shape[2]
    P = u.shape[2]
    T = min(ROW_TILE, S)
    assert S % T == 0 and T % SUB_TILE == 0
    row = lambda b, s: (b, s, 0)
    n_lo = S // T // 2
    assert S % (2 * T) == 0
    grid = (B, S // T)
    cast_in_specs, cast_out_specs, cast_shapes = _cast_specs(casts, grid)
    return pl.pallas_call(
        functools.partial(_merge_kernel, layer=layer, n_lo=n_lo, n_cast=len(casts)),
        grid=grid,
        in_specs=[
            pl.BlockSpec((1, T, D), row),
            pl.BlockSpec((1, T, A), lambda b, s: (b, jnp.minimum(s, n_lo - 1), 0)),
            pl.BlockSpec((1, T, A), lambda b, s: (b, jnp.maximum(s - n_lo, 0), 0)),
            pl.BlockSpec((1, T, P), row),
            _whole(g.shape),
            _layer_operand((D, 2 * D), layer),
            _layer_operand(wa.shape[1:], layer),
            _layer_operand(wp.shape[1:], layer),
            _whole(ps.shape),
            _layer_operand(wb.shape[1:], layer),
            _layer_operand(wo.shape[1:], layer),
            *cast_in_specs,
        ],
        out_specs=[pl.BlockSpec((1, T, D), row), *cast_out_specs],
        out_shape=[jax.ShapeDtypeStruct((B, S, D), x.dtype), *cast_shapes],
        scratch_shapes=[pltpu.VMEM((POOL_HALO + T, P), jnp.float32)] * 4,
        compiler_params=pltpu.CompilerParams(
            dimension_semantics=("parallel", "arbitrary"), vmem_limit_bytes=V7X_VMEM_LIMIT),
        name="merge",
    )(x, a_lo, a_hi, u, g, wg, wa, wp, ps, wb, wo, *(c[0] for c in casts))


def _ffn_kernel(x_ref, g_ref, wup_ref, cw_ref, cb_ref, wdn_ref, gf_ref, o_ref,
                ext_ref, carry_ref, act_ref, *, layer, final_norm):
    T = x_ref.shape[1]
    F = wdn_ref.shape[0]
    FC = FF_CHUNK
    H = CONV_HALO
    s_idx = pl.program_id(1)

    @pl.when(s_idx == 0)
    def _():
        carry_ref[...] = jnp.zeros(carry_ref.shape, jnp.float32)

    h = _rms_norm(x_ref[0], g_ref[layer:layer + 1, :]).astype(jnp.bfloat16)

    def conv_cols(c0, scale):
        a = jnp.dot(h, wup_ref[:, c0:c0 + FC], preferred_element_type=jnp.float32)
        ext_ref[0:H, :] = carry_ref[:, c0:c0 + FC]
        ext_ref[H:H + T, :] = a
        carry_ref[:, c0:c0 + FC] = a[T - H:, :]
        w = cw_ref[:, c0:c0 + FC] * scale
        return (cb_ref[layer:layer + 1, c0:c0 + FC] * scale + w[2:3, :] * a
                + w[1:2, :] * ext_ref[H - 1:H - 1 + T, :]
                + w[0:1, :] * ext_ref[H - 2:H - 2 + T, :])

    for c in range(F // FC):
        half_gate = conv_cols(c * FC, 0.5)
        val = conv_cols(F + c * FC, 1.0)
        act_ref[:, c * FC:(c + 1) * FC] = (half_gate * (jnp.tanh(half_gate) + 1.0) * val).astype(act_ref.dtype)

    y = x_ref[0] + jnp.dot(act_ref[...], wdn_ref[...], preferred_element_type=jnp.float32)
    if final_norm:
        y = _rms_norm(y, gf_ref[...])
    o_ref[0] = y


def _ffn(x, g, wup, cw, cb, wdn, gf, layer, final_norm):
    B, S, D = x.shape
    F = wdn.shape[1]
    T = min(FFN_ROW_TILE, S)
    assert S % T == 0
    row = lambda b, s: (b, s, 0)
    return pl.pallas_call(
        functools.partial(_ffn_kernel, layer=layer, final_norm=final_norm),
        grid=(B, S // T),
        in_specs=[
            pl.BlockSpec((1, T, D), row),
            _whole(g.shape),
            _layer_operand(wup.shape[1:], layer),
            _layer_operand(cw.shape[1:], layer),
            _whole(cb.shape),
            _layer_operand(wdn.shape[1:], layer),
            _whole(gf.shape),
        ],
        out_specs=pl.BlockSpec((1, T, D), row),
        out_shape=jax.ShapeDtypeStruct((B, S, D), x.dtype),
        scratch_shapes=[
            pltpu.VMEM((CONV_HALO + T, FF_CHUNK), jnp.float32),
            pltpu.VMEM((CONV_HALO, 2 * F), jnp.float32),
            pltpu.VMEM((T, F), jnp.bfloat16),
        ],
        compiler_params=pltpu.CompilerParams(
            dimension_semantics=("parallel", "arbitrary"), vmem_limit_bytes=V7X_VMEM_LIMIT),
        name="ffn",
    )(x, g, wup, cw, cb, wdn, gf)


def kernel(x, norm_mix_g, w_in, w_pool, pool_scale, w_branch_a, w_branch_b, w_out, norm_ffn_g, w_up, conv_w, conv_b, w_down, norm_final_g):
    depth = w_in.shape[0]
    D = x.shape[-1]
    A = w_branch_a.shape[1]
    P = w_branch_b.shape[1]
    assert x.shape[1] % MOBA_BLOCK == 0 and A % (2 * HEAD_DIM) == 0
    assert w_down.shape[1] % FF_CHUNK == 0 and 2 * POOL_WINDOWS[-1] <= POOL_HALO and POOL_WINDOWS == (2, 4, 8, 16) and CONV_WIDTH - 1 <= CONV_HALO
    bf = jnp.bfloat16
    assert 3 * A + P == 2 * D, "the branch gates are column block 1 (width 2D) of w_in"
    wvt = jnp.swapaxes(w_in[:, :, 2 * A:3 * A], 1, 2).astype(bf)
    whole = lambda w: (w.reshape(-1, w.shape[-1]), w.shape[-1], 0)
    early = (whole(w_branch_a), whole(w_pool), whole(w_branch_b), whole(w_out),
             (w_in.reshape(-1, w_in.shape[-1]), 2 * D, 1))
    late = (whole(w_up), whole(w_down))
    gf = norm_final_g.reshape(1, D)
    for layer in range(depth):
        q, k, vt, u, *cast = _in_proj(x, norm_mix_g, w_in, wvt, layer, A, P, casts=early if layer == 0 else ())
        if layer == 0:
            wa, wp, wb, wo = (c.reshape(w.shape) for c, w in zip(cast, (w_branch_a, w_pool, w_branch_b, w_out)))
            wg = cast[4].reshape(depth, D, 2 * D)
        a_lo, a_hi = _moba(q, k, vt)
        x, *cast = _merge(x, a_lo, a_hi, u, norm_mix_g, wg, wa, wp, pool_scale, wb, wo, layer,
                          casts=late if layer == 0 else ())
        if layer == 0:
            wup, wdn = (c.reshape(w.shape) for c, w in zip(cast, (w_up, w_down)))
        x = _ffn(x, norm_ffn_g, wup, conv_w, conv_b, wdn, gf, layer, layer == depth - 1)
    return x
```

```python
import functools

import jax
import jax.numpy as jnp
from jax import lax
from jax.experimental import pallas as pl
from jax.experimental.pallas import tpu as pltpu

HEAD_DIM = 64
MOBA_BLOCK = 256
MOBA_TOPK = 3
POOL_WINDOWS = (2, 4, 8, 16)
CONV_WIDTH = 3
RMS_EPS = 1e-6
NEG_INF = -1e30
LOG2_E = 1.4426950408889634

ROW_TILE = 1024
FFN_ROW_TILE = 512
SUB_TILE = 256
POOL_HALO = 32
ONES_ROWS = 16
CONV_HALO = 8
FF_CHUNK = 256
V7X_VMEM_LIMIT = 56 * 1024 * 1024

_NT = (((1,), (1,)), ((), ()))


def _layer_operand(block, layer, index=None):
    index = (0,) * len(block) if index is None else index
    return pl.BlockSpec((None,) + tuple(block), lambda b, s: (layer,) + tuple(index),
                        pipeline_mode=pl.Buffered(1))


def _whole(shape):
    return pl.BlockSpec(shape, lambda b, s: (0,) * len(shape), pipeline_mode=pl.Buffered(1))


def _cast_specs(arrays, grid):
    n_steps = grid[0] * grid[1]
    in_specs, out_specs, shapes = [], [], []
    for arr, cols, col_block in arrays:
        rows = arr.shape[0]
        assert rows % (16 * n_steps) == 0, "bf16 row blocks are 16-row tiles"
        in_specs.append(pl.BlockSpec((rows // n_steps, cols), lambda b, s, j=col_block: (b * grid[1] + s, j)))
        out_specs.append(pl.BlockSpec((rows // n_steps, cols), lambda b, s: (b * grid[1] + s, 0)))
        shapes.append(jax.ShapeDtypeStruct((rows, cols), jnp.bfloat16))
    return in_specs, out_specs, shapes


def _cast_blocks(in_refs, out_refs):
    for src, dst in zip(in_refs, out_refs, strict=True):
        dst[...] = src[...].astype(dst.dtype)


def _rms_norm(x, g):
    y = x * lax.rsqrt(jnp.mean(x * x, axis=-1, keepdims=True) + RMS_EPS)
    return y * g


def _in_proj_kernel(x_ref, g_ref, wq32_ref, wk32_ref, wvt_ref, wu32_ref, *refs, scale, layer, n_cast):
    cast_in, (q_ref, k_ref, vt_ref, u_ref) = refs[:n_cast], refs[n_cast:n_cast + 4]
    cast_out, (wq_ref, wk_ref, wu_ref) = refs[n_cast + 4:2 * n_cast + 4], refs[2 * n_cast + 4:]
    _cast_blocks(cast_in, cast_out)

    @pl.when((pl.program_id(0) == 0) & (pl.program_id(1) == 0))
    def _():
        _cast_blocks((wq32_ref, wk32_ref, wu32_ref), (wq_ref, wk_ref, wu_ref))

    for j in range(vt_ref.shape[1]):
        rows = slice(j * MOBA_BLOCK, (j + 1) * MOBA_BLOCK)
        h = _rms_norm(x_ref[0, rows, :], g_ref[layer:layer + 1, :]).astype(jnp.bfloat16)
        q_ref[0, rows, :] = (jnp.dot(h, wq_ref[...], preferred_element_type=jnp.float32) * scale).astype(q_ref.dtype)
        k_ref[0, rows, :] = jnp.dot(h, wk_ref[...], preferred_element_type=jnp.float32).astype(k_ref.dtype)
        u_ref[0, rows, :] = jnp.dot(h, wu_ref[...], preferred_element_type=jnp.float32).astype(u_ref.dtype)
        vt_ref[0, j] = lax.dot_general(wvt_ref[...], h, _NT,
                                       preferred_element_type=jnp.float32).astype(vt_ref.dtype)


def _in_proj(x, g, w_in, wvt, layer, A, P, casts=()):
    B, S, D = x.shape
    assert P == A, "u is addressed as column block 3 of width A"
    T = min(ROW_TILE, S)
    assert S % T == 0 and T % SUB_TILE == 0
    nb_t = T // MOBA_BLOCK
    nb = S // MOBA_BLOCK
    grid = (B, S // T)
    cast_in_specs, cast_out_specs, cast_shapes = _cast_specs(casts, grid)
    return pl.pallas_call(
        functools.partial(_in_proj_kernel, scale=HEAD_DIM ** -0.5 * LOG2_E, layer=layer, n_cast=len(casts)),
        grid=grid,
        in_specs=[
            pl.BlockSpec((1, T, D), lambda b, s: (b, s, 0)),
            _whole(g.shape),
            _layer_operand((D, A), layer, (0, 0)),
            _layer_operand((D, A), layer, (0, 1)),
            _layer_operand((A, D), layer),
            _layer_operand((D, P), layer, (0, 3)),
            *cast_in_specs,
        ],
        out_specs=[
            pl.BlockSpec((1, T, A), lambda b, s: (b, s, 0)),
            pl.BlockSpec((1, T, A), lambda b, s: (b, s, 0)),
            pl.BlockSpec((1, nb_t, A, MOBA_BLOCK), lambda b, s: (b, s, 0, 0)),
            pl.BlockSpec((1, T, P), lambda b, s: (b, s, 0)),
            *cast_out_specs,
        ],
        out_shape=[
            jax.ShapeDtypeStruct((B, S, A), jnp.bfloat16),
            jax.ShapeDtypeStruct((B, S, A), jnp.bfloat16),
            jax.ShapeDtypeStruct((B, nb, A, MOBA_BLOCK), jnp.bfloat16),
            jax.ShapeDtypeStruct((B, S, P), jnp.bfloat16),
            *cast_shapes,
        ],
        scratch_shapes=[pltpu.VMEM((D, A), jnp.bfloat16), pltpu.VMEM((D, A), jnp.bfloat16),
                        pltpu.VMEM((D, P), jnp.bfloat16)],
        compiler_params=pltpu.CompilerParams(
            dimension_semantics=("arbitrary", "arbitrary"), vmem_limit_bytes=V7X_VMEM_LIMIT),
        name="in_proj",
    )(x, g, w_in, w_in, wvt, w_in, *(c[0] for c in casts))


def _moba_kernel(qa_ref, qb_ref, k_ref, vt_ref, oa_ref, ob_ref, kmean_ref, qza_ref, qzb_ref,
                 s0_ref, mb0_ref, s1_ref, mb1_ref, ota_ref, otb_ref, *, n_heads, n_blocks):
    L = MOBA_BLOCK
    G = 2 * HEAD_DIM
    pair = pl.program_id(1)

    @pl.when(pair == 0)
    def _():
        row = lax.broadcasted_iota(jnp.int32, (n_blocks, n_blocks * L), 0)
        col = lax.broadcasted_iota(jnp.int32, (n_blocks, n_blocks * L), 1)
        ind = jnp.where((col >= row * L) & (col < (row + 1) * L), 1.0 / L, 0.0).astype(jnp.bfloat16)
        k_all = k_ref[0].reshape(n_blocks * L, k_ref.shape[-1])
        kmean_ref[...] = jnp.dot(ind, k_all, preferred_element_type=jnp.float32)

    blk = lax.broadcasted_iota(jnp.int32, (n_blocks, L), 0)
    key_pos = lax.broadcasted_iota(jnp.int32, (L, L), 0)
    qry_pos = lax.broadcasted_iota(jnp.int32, (L, L), 1)
    lane = lax.broadcasted_iota(jnp.int32, (L, G), 1)
    ones_rows = jnp.ones((ONES_ROWS, L), jnp.bfloat16)
    groups = [slice((h // 2) * G, (h // 2 + 1) * G) for h in range(n_heads)]
    slots = ((s0_ref, mb0_ref), (s1_ref, mb1_ref))

    def pv(b, h, p):
        lhs = jnp.concatenate([vt_ref[0, b, h * HEAD_DIM:(h + 1) * HEAD_DIM, :], ones_rows], axis=0)
        return jnp.dot(lhs, p.astype(jnp.bfloat16), preferred_element_type=jnp.float32)

    def stage_a(qz_ref, b, h, slot, causal):
        s_ref, mb_ref = slot
        s = lax.dot_general(k_ref[0, b, :, groups[h]], qz_ref[h], _NT, preferred_element_type=jnp.float32)
        if causal:
            s = jnp.where(key_pos <= qry_pos, s, NEG_INF)
        s_ref[h] = s
        mb_ref[h:h + 1, :] = jnp.max(s, axis=0, keepdims=True)

    def stage_b(b, h, slot, state, on):
        s_ref, mb_ref = slot
        mb = mb_ref[h:h + 1, :]
        if state is None:
            return mb, pv(b, h, jnp.exp2(s_ref[h] - mb))
        m, acc = state
        if on is None:
            m_new = jnp.maximum(m, mb)
            shift = m_new
        else:
            m_new = jnp.where(on, jnp.maximum(m, mb), m)
            shift = jnp.where(on, m_new, -NEG_INF)
        return m_new, jnp.exp2(m - m_new) * acc + pv(b, h, jnp.exp2(s_ref[h] - shift))

    def prepare(ti, q_ref, qz_ref):
        sel = []
        for h in range(n_heads):
            in_head = (lane >= (h % 2) * HEAD_DIM) & (lane < (h % 2 + 1) * HEAD_DIM)
            qz_ref[h] = jnp.where(in_head, q_ref[0, :, groups[h]], jnp.zeros((L, G), q_ref.dtype))
            if ti <= MOBA_TOPK:
                sel.append(None)
                continue
            km = kmean_ref[:, groups[h]]
            km_hi = km.astype(jnp.bfloat16)
            km_lo = (km - km_hi.astype(jnp.float32)).astype(jnp.bfloat16)
            g2 = lax.dot_general(jnp.concatenate([km_hi, km_lo], axis=0), qz_ref[h], _NT,
                                 preferred_element_type=jnp.float32)
            gate = g2[:n_blocks] + g2[n_blocks:]
            rank = jnp.zeros((n_blocks, L), jnp.float32)
            for j in range(ti):
                gj = gate[j:j + 1, :]
                rank = rank + jnp.where((gj > gate) | ((gj == gate) & (blk > j)), 1.0, 0.0)
            sel.append(jnp.where(rank < MOBA_TOPK, 1.0, 0.0))
        return sel

    def finalize(state, ot_ref, o_ref):
        for h in range(n_heads):
            acc = state[h][1]
            inv_l = 1.0 / acc[HEAD_DIM:HEAD_DIM + 1, :]
            o_ref[0, 0, h * HEAD_DIM:(h + 1) * HEAD_DIM, :] = (acc[0:HEAD_DIM, :] * inv_l).astype(o_ref.dtype)

    def tile_pair(tp):
        tiles = ((tp, qa_ref, qza_ref, ota_ref, oa_ref), (n_blocks - 1 - tp, qb_ref, qzb_ref, otb_ref, ob_ref))
        sels = [prepare(ti, q_ref, qz_ref) for ti, q_ref, qz_ref, _, _ in tiles]
        visits = [(w, b, n == 0, n == tiles[w][0]) for w in range(2)
                  for n, b in enumerate([tiles[w][0]] + list(range(tiles[w][0])))]
        states = [[None] * n_heads, [None] * n_heads]
        for h in range(n_heads):
            stage_a(tiles[0][2], visits[0][1], h, slots[0], causal=True)
        for n, (w, b, first, last) in enumerate(visits):
            for h in range(n_heads):
                if n + 1 < len(visits):
                    w2, b2, first2, _ = visits[n + 1]
                    stage_a(tiles[w2][2], b2, h, slots[(n + 1) % 2], causal=first2)
                on = None if (first or sels[w][h] is None) else sels[w][h][b:b + 1, :] > 0.0
                states[w][h] = stage_b(b, h, slots[n % 2], states[w][h], on)
            if last:
                finalize(states[w], tiles[w][3], tiles[w][4])

    for tp in range(n_blocks // 2):
        pl.when(pair == tp)(functools.partial(tile_pair, tp))


def _moba(q, k, vt):
    B, S, A = q.shape
    L = MOBA_BLOCK
    nb = S // L
    H = A // HEAD_DIM
    assert nb % 2 == 0
    k4 = k.reshape(B, nb, L, A)
    half = jax.ShapeDtypeStruct((B, nb // 2, A, L), jnp.bfloat16)
    return pl.pallas_call(
        functools.partial(_moba_kernel, n_heads=H, n_blocks=nb),
        grid=(B, nb // 2),
        in_specs=[
            pl.BlockSpec((1, L, A), lambda b, j: (b, j, 0)),
            pl.BlockSpec((1, L, A), lambda b, j: (b, nb - 1 - j, 0)),
            pl.BlockSpec((1, nb, L, A), lambda b, j: (b, 0, 0, 0)),
            pl.BlockSpec((1, nb, A, L), lambda b, j: (b, 0, 0, 0)),
        ],
        out_specs=[
            pl.BlockSpec((1, 1, A, L), lambda b, j: (b, j, 0, 0)),
            pl.BlockSpec((1, 1, A, L), lambda b, j: (b, nb // 2 - 1 - j, 0, 0)),
        ],
        out_shape=[half, half],
        scratch_shapes=[
            pltpu.VMEM((nb, A), jnp.float32),
            pltpu.VMEM((H, L, 2 * HEAD_DIM), jnp.bfloat16),
            pltpu.VMEM((H, L, 2 * HEAD_DIM), jnp.bfloat16),
            pltpu.VMEM((H, L, L), jnp.float32),
            pltpu.VMEM((H, L), jnp.float32),
            pltpu.VMEM((H, L, L), jnp.float32),
            pltpu.VMEM((H, L), jnp.float32),
            pltpu.VMEM((A, L), jnp.bfloat16),
            pltpu.VMEM((A, L), jnp.bfloat16),
        ],
        compiler_params=pltpu.CompilerParams(
            dimension_semantics=("parallel", "arbitrary"), vmem_limit_bytes=V7X_VMEM_LIMIT),
        name="moba",
    )(q, q, k4, vt)


def _sigmoid(x):
    return 0.5 * jnp.tanh(0.5 * x) + 0.5


def _merge_kernel(x_ref, a_lo_ref, a_hi_ref, u_ref, g_ref, wg_ref, wa_ref, wp_ref, ps_ref, wb_ref, wo_ref,
                  *refs, layer, n_lo, n_cast):
    cast_in, o_ref, cast_out = refs[:n_cast], refs[n_cast], refs[n_cast + 1:2 * n_cast + 1]
    uext_ref, w2_ref, w4_ref, w8_ref = refs[2 * n_cast + 1:]
    _cast_blocks(cast_in, cast_out)
    T = x_ref.shape[1]
    D = x_ref.shape[2]
    GW = wp_ref.shape[1]
    R = SUB_TILE
    HALO = POOL_HALO
    s_idx = pl.program_id(1)

    @pl.when(s_idx == 0)
    def _():
        uext_ref[0:HALO, :] = jnp.zeros((HALO, uext_ref.shape[1]), jnp.float32)

    def branch_in(r0):
        at = jnp.where(s_idx < n_lo, a_lo_ref[0, r0 // R], a_hi_ref[0, r0 // R])
        y_a = lax.dot_general(at, wa_ref[...], (((0,), (0,)), ((), ())), preferred_element_type=jnp.float32)
        h = _rms_norm(x_ref[0, r0:r0 + R, :], g_ref[layer:layer + 1, :]).astype(jnp.bfloat16)
        return y_a, jnp.dot(h, wg_ref[...], preferred_element_type=jnp.float32)

    def window_sums():
        n = HALO + T
        uext_ref[HALO:n, :] = u_ref[0].astype(jnp.float32)
        w2_ref[8:n, :] = uext_ref[8:n, :] + uext_ref[7:n - 1, :]
        w4_ref[16:n, GW:] = w2_ref[16:n, GW:] + w2_ref[14:n - 2, GW:]
        w8_ref[24:n, 2 * GW:] = w4_ref[24:n, 2 * GW:] + w4_ref[20:n - 4, 2 * GW:]

    def branch_out(r0, y_a, g_pre):
        e0 = HALO + r0
        wsums = (w2_ref[e0:e0 + R, 0:GW], w4_ref[e0:e0 + R, GW:2 * GW], w8_ref[e0:e0 + R, 2 * GW:3 * GW],
                 w8_ref[e0:e0 + R, 3 * GW:] + w8_ref[e0 - 8:e0 - 8 + R, 3 * GW:])
        t_pos = s_idx * T + r0 + lax.broadcasted_iota(jnp.int32, (R, GW), 0)
        ys = []
        for gi, win in enumerate(POOL_WINDOWS):
            count = jnp.minimum(t_pos + 1, win).astype(jnp.float32)
            mixed = (wsums[gi] / count - uext_ref[e0:e0 + R, gi * GW:(gi + 1) * GW]).astype(jnp.bfloat16)
            ys.append(jnp.dot(mixed, wp_ref[gi], preferred_element_type=jnp.float32))
        pooled = (jnp.concatenate(ys, axis=1) * ps_ref[layer:layer + 1, :]).astype(jnp.bfloat16)
        y_b = jnp.dot(pooled, wb_ref[...], preferred_element_type=jnp.float32)
        gates = _sigmoid(g_pre)
        merged = (gates[:, :D] * y_a + gates[:, D:] * y_b).astype(jnp.bfloat16)
        o_ref[0, r0:r0 + R, :] = x_ref[0, r0:r0 + R, :] + jnp.dot(merged, wo_ref[...],
                                                                    preferred_element_type=jnp.float32)

    n_sub = T // R
    pending = branch_in(0)
    window_sums()
    for j in range(n_sub):
        nxt = branch_in((j + 1) * R) if j + 1 < n_sub else None
        branch_out(j * R, *pending)
        pending = nxt
    uext_ref[0:HALO, :] = uext_ref[T:T + HALO, :]


def _merge(x, a_lo, a_hi, u, g, wg, wa, wp, ps, wb, wo, layer, casts=()):
    B, S, D = x.shape
    A = a_lo.shape[2]
    assert a_lo.shape[3] == SUB_TILE
    P = u.shape[2]
    T = min(ROW_TILE, S)
    assert S % T == 0 and T % SUB_TILE == 0
    row = lambda b, s: (b, s, 0)
    n_lo = S // T // 2
    assert S % (2 * T) == 0
    grid = (B, S // T)
    cast_in_specs, cast_out_specs, cast_shapes = _cast_specs(casts, grid)
    return pl.pallas_call(
        functools.partial(_merge_kernel, layer=layer, n_lo=n_lo, n_cast=len(casts)),
        grid=grid,
        in_specs=[
            pl.BlockSpec((1, T, D), row),
            pl.BlockSpec((1, T // SUB_TILE, A, SUB_TILE), lambda b, s: (b, jnp.minimum(s, n_lo - 1), 0, 0)),
            pl.BlockSpec((1, T // SUB_TILE, A, SUB_TILE), lambda b, s: (b, jnp.maximum(s - n_lo, 0), 0, 0)),
            pl.BlockSpec((1, T, P), row),
            _whole(g.shape),
            _layer_operand((D, 2 * D), layer),
            _layer_operand(wa.shape[1:], layer),
            _layer_operand(wp.shape[1:], layer),
            _whole(ps.shape),
            _layer_operand(wb.shape[1:], layer),
            _layer_operand(wo.shape[1:], layer),
            *cast_in_specs,
        ],
        out_specs=[pl.BlockSpec((1, T, D), row), *cast_out_specs],
        out_shape=[jax.ShapeDtypeStruct((B, S, D), x.dtype), *cast_shapes],
        scratch_shapes=[pltpu.VMEM((POOL_HALO + T, P), jnp.float32)] * 4,
        compiler_params=pltpu.CompilerParams(
            dimension_semantics=("parallel", "arbitrary"), vmem_limit_bytes=V7X_VMEM_LIMIT),
        name="merge",
    )(x, a_lo, a_hi, u, g, wg, wa, wp, ps, wb, wo, *(c[0] for c in casts))


def _ffn_kernel(x_ref, g_ref, wup_ref, cw_ref, cb_ref, wdn_ref, gf_ref, o_ref,
                ext_ref, carry_ref, act_ref, *, layer, final_norm):
    T = x_ref.shape[1]
    F = wdn_ref.shape[0]
    FC = FF_CHUNK
    H = CONV_HALO
    s_idx = pl.program_id(1)

    @pl.when(s_idx == 0)
    def _():
        carry_ref[...] = jnp.zeros(carry_ref.shape, jnp.float32)

    h = _rms_norm(x_ref[0], g_ref[layer:layer + 1, :]).astype(jnp.bfloat16)

    def conv_cols(c0, scale):
        a = jnp.dot(h, wup_ref[:, c0:c0 + FC], preferred_element_type=jnp.float32)
        ext_ref[0:H, :] = carry_ref[:, c0:c0 + FC]
        ext_ref[H:H + T, :] = a
        carry_ref[:, c0:c0 + FC] = a[T - H:, :]
        w = cw_ref[:, c0:c0 + FC] * scale
        return (cb_ref[layer:layer + 1, c0:c0 + FC] * scale + w[2:3, :] * a
                + w[1:2, :] * ext_ref[H - 1:H - 1 + T, :]
                + w[0:1, :] * ext_ref[H - 2:H - 2 + T, :])

    for c in range(F // FC):
        half_gate = conv_cols(c * FC, 0.5)
        val = conv_cols(F + c * FC, 1.0)
        act_ref[:, c * FC:(c + 1) * FC] = (half_gate * (jnp.tanh(half_gate) + 1.0) * val).astype(act_ref.dtype)

    y = x_ref[0] + jnp.dot(act_ref[...], wdn_ref[...], preferred_element_type=jnp.float32)
    if final_norm:
        y = _rms_norm(y, gf_ref[...])
    o_ref[0] = y


def _ffn(x, g, wup, cw, cb, wdn, gf, layer, final_norm):
    B, S, D = x.shape
    F = wdn.shape[1]
    T = min(FFN_ROW_TILE, S)
    assert S % T == 0
    row = lambda b, s: (b, s, 0)
    return pl.pallas_call(
        functools.partial(_ffn_kernel, layer=layer, final_norm=final_norm),
        grid=(B, S // T),
        in_specs=[
            pl.BlockSpec((1, T, D), row),
            _whole(g.shape),
            _layer_operand(wup.shape[1:], layer),
            _layer_operand(cw.shape[1:], layer),
            _whole(cb.shape),
            _layer_operand(wdn.shape[1:], layer),
            _whole(gf.shape),
        ],
        out_specs=pl.BlockSpec((1, T, D), row),
        out_shape=jax.ShapeDtypeStruct((B, S, D), x.dtype),
        scratch_shapes=[
            pltpu.VMEM((CONV_HALO + T, FF_CHUNK), jnp.float32),
            pltpu.VMEM((CONV_HALO, 2 * F), jnp.float32),
            pltpu.VMEM((T, F), jnp.bfloat16),
        ],
        compiler_params=pltpu.CompilerParams(
            dimension_semantics=("parallel", "arbitrary"), vmem_limit_bytes=V7X_VMEM_LIMIT),
        name="ffn",
    )(x, g, wup, cw, cb, wdn, gf)


def kernel(x, norm_mix_g, w_in, w_pool, pool_scale, w_branch_a, w_branch_b, w_out, norm_ffn_g, w_up, conv_w, conv_b, w_down, norm_final_g):
    depth = w_in.shape[0]
    D = x.shape[-1]
    A = w_branch_a.shape[1]
    P = w_branch_b.shape[1]
    assert x.shape[1] % MOBA_BLOCK == 0 and A % (2 * HEAD_DIM) == 0
    assert w_down.shape[1] % FF_CHUNK == 0 and 2 * POOL_WINDOWS[-1] <= POOL_HALO and POOL_WINDOWS == (2, 4, 8, 16) and CONV_WIDTH - 1 <= CONV_HALO
    bf = jnp.bfloat16
    assert 3 * A + P == 2 * D, "the branch gates are column block 1 (width 2D) of w_in"
    wvt = jnp.swapaxes(w_in[:, :, 2 * A:3 * A], 1, 2).astype(bf)
    whole = lambda w: (w.reshape(-1, w.shape[-1]), w.shape[-1], 0)
    early = (whole(w_branch_a), whole(w_pool), whole(w_branch_b), whole(w_out),
             (w_in.reshape(-1, w_in.shape[-1]), 2 * D, 1))
    late = (whole(w_up), whole(w_down))
    gf = norm_final_g.reshape(1, D)
    for layer in range(depth):
        q, k, vt, u, *cast = _in_proj(x, norm_mix_g, w_in, wvt, layer, A, P, casts=early if layer == 0 else ())
        if layer == 0:
            wa, wp, wb, wo = (c.reshape(w.shape) for c, w in zip(cast, (w_branch_a, w_pool, w_branch_b, w_out)))
            wg = cast[4].reshape(depth, D, 2 * D)
        a_lo, a_hi = _moba(q, k, vt)
        x, *cast = _merge(x, a_lo, a_hi, u, norm_mix_g, wg, wa, wp, pool_scale, wb, wo, layer,
                          casts=late if layer == 0 else ())
        if layer == 0:
            wup, wdn = (c.reshape(w.shape) for c, w in zip(cast, (w_up, w_down)))
        x = _ffn(x, norm_ffn_g, wup, conv_w, conv_b, wdn, gf, layer, layer == depth - 1)
    return x
```

```python
import functools

import jax
import jax.numpy as jnp
from jax import lax
from jax.experimental import pallas as pl
from jax.experimental.pallas import tpu as pltpu

HEAD_DIM = 64
MOBA_BLOCK = 256
MOBA_TOPK = 3
POOL_WINDOWS = (2, 4, 8, 16)
CONV_WIDTH = 3
RMS_EPS = 1e-6
NEG_INF = -1e30
LOG2_E = 1.4426950408889634

ROW_TILE = 1024
FFN_ROW_TILE = 512
SUB_TILE = 256
POOL_HALO = 32
ONES_ROWS = 16
CONV_HALO = 8
FF_CHUNK = 256
V7X_VMEM_LIMIT = 56 * 1024 * 1024

_NT = (((1,), (1,)), ((), ()))


def _layer_operand(block, layer, index=None):
    index = (0,) * len(block) if index is None else index
    return pl.BlockSpec((None,) + tuple(block), lambda b, s: (layer,) + tuple(index),
                        pipeline_mode=pl.Buffered(1))


def _whole(shape):
    return pl.BlockSpec(shape, lambda b, s: (0,) * len(shape), pipeline_mode=pl.Buffered(1))


def _cast_specs(arrays, grid):
    n_steps = grid[0] * grid[1]
    in_specs, out_specs, shapes = [], [], []
    for arr, cols, col_block in arrays:
        rows = arr.shape[0]
        assert rows % (16 * n_steps) == 0, "bf16 row blocks are 16-row tiles"
        in_specs.append(pl.BlockSpec((rows // n_steps, cols), lambda b, s, j=col_block: (b * grid[1] + s, j)))
        out_specs.append(pl.BlockSpec((rows // n_steps, cols), lambda b, s: (b * grid[1] + s, 0)))
        shapes.append(jax.ShapeDtypeStruct((rows, cols), jnp.bfloat16))
    return in_specs, out_specs, shapes


def _cast_blocks(in_refs, out_refs):
    for src, dst in zip(in_refs, out_refs, strict=True):
        dst[...] = src[...].astype(dst.dtype)


def _rms_norm(x, g):
    y = x * lax.rsqrt(jnp.mean(x * x, axis=-1, keepdims=True) + RMS_EPS)
    return y * g


def _in_proj_kernel(x_ref, g_ref, wq32_ref, wk32_ref, wv32_ref, wu32_ref, *refs, scale, layer, n_cast):
    cast_in, (q_ref, k_ref, vt_ref, u_ref) = refs[:n_cast], refs[n_cast:n_cast + 4]
    cast_out, (wq_ref, wk_ref, wvt_ref, wu_ref) = refs[n_cast + 4:2 * n_cast + 4], refs[2 * n_cast + 4:]
    _cast_blocks(cast_in, cast_out)

    @pl.when((pl.program_id(0) == 0) & (pl.program_id(1) == 0))
    def _():
        _cast_blocks((wq32_ref, wk32_ref, wu32_ref), (wq_ref, wk_ref, wu_ref))
        wvt_ref[...] = wv32_ref[...].T.astype(wvt_ref.dtype)

    for j in range(vt_ref.shape[1]):
        rows = slice(j * MOBA_BLOCK, (j + 1) * MOBA_BLOCK)
        h = _rms_norm(x_ref[0, rows, :], g_ref[layer:layer + 1, :]).astype(jnp.bfloat16)
        q_ref[0, rows, :] = (jnp.dot(h, wq_ref[...], preferred_element_type=jnp.float32) * scale).astype(q_ref.dtype)
        k_ref[0, rows, :] = jnp.dot(h, wk_ref[...], preferred_element_type=jnp.float32).astype(k_ref.dtype)
        u_ref[0, rows, :] = jnp.dot(h, wu_ref[...], preferred_element_type=jnp.float32).astype(u_ref.dtype)
        vt_ref[0, j] = lax.dot_general(wvt_ref[...], h, _NT,
                                       preferred_element_type=jnp.float32).astype(vt_ref.dtype)


def _in_proj(x, g, w_in, layer, A, P, casts=()):
    B, S, D = x.shape
    assert P == A, "u is addressed as column block 3 of width A"
    T = min(ROW_TILE, S)
    assert S % T == 0 and T % SUB_TILE == 0
    nb_t = T // MOBA_BLOCK
    nb = S // MOBA_BLOCK
    grid = (B, S // T)
    cast_in_specs, cast_out_specs, cast_shapes = _cast_specs(casts, grid)
    return pl.pallas_call(
        functools.partial(_in_proj_kernel, scale=HEAD_DIM ** -0.5 * LOG2_E, layer=layer, n_cast=len(casts)),
        grid=grid,
        in_specs=[
            pl.BlockSpec((1, T, D), lambda b, s: (b, s, 0)),
            _whole(g.shape),
            _layer_operand((D, A), layer, (0, 0)),
            _layer_operand((D, A), layer, (0, 1)),
            _layer_operand((D, A), layer, (0, 2)),
            _layer_operand((D, P), layer, (0, 3)),
            *cast_in_specs,
        ],
        out_specs=[
            pl.BlockSpec((1, T, A), lambda b, s: (b, s, 0)),
            pl.BlockSpec((1, T, A), lambda b, s: (b, s, 0)),
            pl.BlockSpec((1, nb_t, A, MOBA_BLOCK), lambda b, s: (b, s, 0, 0)),
            pl.BlockSpec((1, T, P), lambda b, s: (b, s, 0)),
            *cast_out_specs,
        ],
        out_shape=[
            jax.ShapeDtypeStruct((B, S, A), jnp.bfloat16),
            jax.ShapeDtypeStruct((B, S, A), jnp.bfloat16),
            jax.ShapeDtypeStruct((B, nb, A, MOBA_BLOCK), jnp.bfloat16),
            jax.ShapeDtypeStruct((B, S, P), jnp.bfloat16),
            *cast_shapes,
        ],
        scratch_shapes=[pltpu.VMEM((D, A), jnp.bfloat16), pltpu.VMEM((D, A), jnp.bfloat16),
                        pltpu.VMEM((A, D), jnp.bfloat16), pltpu.VMEM((D, P), jnp.bfloat16)],
        compiler_params=pltpu.CompilerParams(
            dimension_semantics=("arbitrary", "arbitrary"), vmem_limit_bytes=V7X_VMEM_LIMIT),
        name="in_proj",
    )(x, g, w_in, w_in, w_in, w_in, *(c[0] for c in casts))


def _moba_kernel(qa_ref, qb_ref, k_ref, vt_ref, oa_ref, ob_ref, kmean_ref, qza_ref, qzb_ref,
                 s0_ref, mb0_ref, s1_ref, mb1_ref, *, n_heads, n_blocks):
    L = MOBA_BLOCK
    G = 2 * HEAD_DIM
    pair = pl.program_id(1)

    @pl.when(pair == 0)
    def _():
        row = lax.broadcasted_iota(jnp.int32, (n_blocks, n_blocks * L), 0)
        col = lax.broadcasted_iota(jnp.int32, (n_blocks, n_blocks * L), 1)
        ind = jnp.where((col >= row * L) & (col < (row + 1) * L), 1.0 / L, 0.0).astype(jnp.bfloat16)
        k_all = k_ref[0].reshape(n_blocks * L, k_ref.shape[-1])
        kmean_ref[...] = jnp.dot(ind, k_all, preferred_element_type=jnp.float32)

    blk = lax.broadcasted_iota(jnp.int32, (n_blocks, L), 0)
    key_pos = lax.broadcasted_iota(jnp.int32, (L, L), 0)
    qry_pos = lax.broadcasted_iota(jnp.int32, (L, L), 1)
    lane = lax.broadcasted_iota(jnp.int32, (L, G), 1)
    ones_rows = jnp.ones((ONES_ROWS, L), jnp.bfloat16)
    groups = [slice((h // 2) * G, (h // 2 + 1) * G) for h in range(n_heads)]
    slots = ((s0_ref, mb0_ref), (s1_ref, mb1_ref))

    def pv(b, h, p):
        lhs = jnp.concatenate([vt_ref[0, b, h * HEAD_DIM:(h + 1) * HEAD_DIM, :], ones_rows], axis=0)
        return jnp.dot(lhs, p.astype(jnp.bfloat16), preferred_element_type=jnp.float32)

    def stage_a(qz_ref, b, h, slot, causal):
        s_ref, mb_ref = slot
        s = lax.dot_general(k_ref[0, b, :, groups[h]], qz_ref[h], _NT, preferred_element_type=jnp.float32)
        if causal:
            s = jnp.where(key_pos <= qry_pos, s, NEG_INF)
        s_ref[h] = s
        mb_ref[h:h + 1, :] = jnp.max(s, axis=0, keepdims=True)

    def stage_b(b, h, slot, state, on):
        s_ref, mb_ref = slot
        mb = mb_ref[h:h + 1, :]
        if state is None:
            return mb, pv(b, h, jnp.exp2(s_ref[h] - mb))
        m, acc = state
        if on is None:
            m_new = jnp.maximum(m, mb)
            shift = m_new
        else:
            m_new = jnp.where(on, jnp.maximum(m, mb), m)
            shift = jnp.where(on, m_new, -NEG_INF)
        return m_new, jnp.exp2(m - m_new) * acc + pv(b, h, jnp.exp2(s_ref[h] - shift))

    def prepare(ti, q_ref, qz_ref):
        sel = []
        for h in range(n_heads):
            in_head = (lane >= (h % 2) * HEAD_DIM) & (lane < (h % 2 + 1) * HEAD_DIM)
            qz_ref[h] = jnp.where(in_head, q_ref[0, :, groups[h]], jnp.zeros((L, G), q_ref.dtype))
            if ti <= MOBA_TOPK:
                sel.append(None)
                continue
            km = kmean_ref[:, groups[h]]
            km_hi = km.astype(jnp.bfloat16)
            km_lo = (km - km_hi.astype(jnp.float32)).astype(jnp.bfloat16)
            g2 = lax.dot_general(jnp.concatenate([km_hi, km_lo], axis=0), qz_ref[h], _NT,
                                 preferred_element_type=jnp.float32)
            gate = g2[:n_blocks] + g2[n_blocks:]
            rank = jnp.zeros((n_blocks, L), jnp.float32)
            for j in range(ti):
                gj = gate[j:j + 1, :]
                rank = rank + jnp.where((gj > gate) | ((gj == gate) & (blk > j)), 1.0, 0.0)
            sel.append(jnp.where(rank < MOBA_TOPK, 1.0, 0.0))
        return sel

    def finalize(state, o_ref):
        for h in range(n_heads):
            acc = state[h][1]
            inv_l = 1.0 / acc[HEAD_DIM:HEAD_DIM + 1, :]
            o_ref[0, 0, h * HEAD_DIM:(h + 1) * HEAD_DIM, :] = (acc[0:HEAD_DIM, :] * inv_l).astype(o_ref.dtype)

    def tile_pair(tp):
        tiles = ((tp, qa_ref, qza_ref, oa_ref), (n_blocks - 1 - tp, qb_ref, qzb_ref, ob_ref))
        sels = [prepare(ti, q_ref, qz_ref) for ti, q_ref, qz_ref, _ in tiles]
        visits = [(w, b, n == 0, n == tiles[w][0]) for w in range(2)
                  for n, b in enumerate([tiles[w][0]] + list(range(tiles[w][0])))]
        states = [[None] * n_heads, [None] * n_heads]
        for h in range(n_heads):
            stage_a(tiles[0][2], visits[0][1], h, slots[0], causal=True)
        for n, (w, b, first, last) in enumerate(visits):
            for h in range(n_heads):
                if n + 1 < len(visits):
                    w2, b2, first2, _ = visits[n + 1]
                    stage_a(tiles[w2][2], b2, h, slots[(n + 1) % 2], causal=first2)
                on = None if (first or sels[w][h] is None) else sels[w][h][b:b + 1, :] > 0.0
                states[w][h] = stage_b(b, h, slots[n % 2], states[w][h], on)
            if last:
                finalize(states[w], tiles[w][3])

    for tp in range(n_blocks // 2):
        pl.when(pair == tp)(functools.partial(tile_pair, tp))


def _moba(q, k, vt):
    B, S, A = q.shape
    L = MOBA_BLOCK
    nb = S // L
    H = A // HEAD_DIM
    assert nb % 2 == 0
    k4 = k.reshape(B, nb, L, A)
    half = jax.ShapeDtypeStruct((B, nb // 2, A, L), jnp.bfloat16)
    return pl.pallas_call(
        functools.partial(_moba_kernel, n_heads=H, n_blocks=nb),
        grid=(B, nb // 2),
        in_specs=[
            pl.BlockSpec((1, L, A), lambda b, j: (b, j, 0)),
            pl.BlockSpec((1, L, A), lambda b, j: (b, nb - 1 - j, 0)),
            pl.BlockSpec((1, nb, L, A), lambda b, j: (b, 0, 0, 0)),
            pl.BlockSpec((1, nb, A, L), lambda b, j: (b, 0, 0, 0)),
        ],
        out_specs=[
            pl.BlockSpec((1, 1, A, L), lambda b, j: (b, j, 0, 0)),
            pl.BlockSpec((1, 1, A, L), lambda b, j: (b, nb // 2 - 1 - j, 0, 0)),
        ],
        out_shape=[half, half],
        scratch_shapes=[
            pltpu.VMEM((nb, A), jnp.float32),
            pltpu.VMEM((H, L, 2 * HEAD_DIM), jnp.bfloat16),
            pltpu.VMEM((H, L, 2 * HEAD_DIM), jnp.bfloat16),
            pltpu.VMEM((H, L, L), jnp.float32),
            pltpu.VMEM((H, L), jnp.float32),
            pltpu.VMEM((H, L, L), jnp.float32),
            pltpu.VMEM((H, L), jnp.float32),
        ],
        compiler_params=pltpu.CompilerParams(
            dimension_semantics=("parallel", "arbitrary"), vmem_limit_bytes=V7X_VMEM_LIMIT),
        name="moba",
    )(q, q, k4, vt)


def _sigmoid(x):
    return 0.5 * jnp.tanh(0.5 * x) + 0.5


def _merge_kernel(x_ref, a_lo_ref, a_hi_ref, u_ref, g_ref, wg_ref, wa_ref, wp_ref, ps_ref, wb_ref, wo_ref,
                  *refs, layer, n_lo, n_cast):
    cast_in, o_ref, cast_out = refs[:n_cast], refs[n_cast], refs[n_cast + 1:2 * n_cast + 1]
    uext_ref, w2_ref, w4_ref, w8_ref = refs[2 * n_cast + 1:]
    _cast_blocks(cast_in, cast_out)
    T = x_ref.shape[1]
    D = x_ref.shape[2]
    GW = wp_ref.shape[1]
    R = SUB_TILE
    HALO = POOL_HALO
    s_idx = pl.program_id(1)

    @pl.when(s_idx == 0)
    def _():
        uext_ref[0:HALO, :] = jnp.zeros((HALO, uext_ref.shape[1]), jnp.float32)

    def branch_in(r0):
        at = jnp.where(s_idx < n_lo, a_lo_ref[0, r0 // R], a_hi_ref[0, r0 // R])
        y_a = lax.dot_general(at, wa_ref[...], (((0,), (0,)), ((), ())), preferred_element_type=jnp.float32)
        h = _rms_norm(x_ref[0, r0:r0 + R, :], g_ref[layer:layer + 1, :]).astype(jnp.bfloat16)
        return y_a, jnp.dot(h, wg_ref[...], preferred_element_type=jnp.float32)

    def window_sums():
        n = HALO + T
        uext_ref[HALO:n, :] = u_ref[0].astype(jnp.float32)
        w2_ref[8:n, :] = uext_ref[8:n, :] + uext_ref[7:n - 1, :]
        w4_ref[16:n, GW:] = w2_ref[16:n, GW:] + w2_ref[14:n - 2, GW:]
        w8_ref[24:n, 2 * GW:] = w4_ref[24:n, 2 * GW:] + w4_ref[20:n - 4, 2 * GW:]

    def branch_out(r0, y_a, g_pre):
        e0 = HALO + r0
        wsums = (w2_ref[e0:e0 + R, 0:GW], w4_ref[e0:e0 + R, GW:2 * GW], w8_ref[e0:e0 + R, 2 * GW:3 * GW],
                 w8_ref[e0:e0 + R, 3 * GW:] + w8_ref[e0 - 8:e0 - 8 + R, 3 * GW:])
        t_pos = s_idx * T + r0 + lax.broadcasted_iota(jnp.int32, (R, GW), 0)
        ys = []
        for gi, win in enumerate(POOL_WINDOWS):
            count = jnp.minimum(t_pos + 1, win).astype(jnp.float32)
            mixed = (wsums[gi] / count - uext_ref[e0:e0 + R, gi * GW:(gi + 1) * GW]).astype(jnp.bfloat16)
            ys.append(jnp.dot(mixed, wp_ref[gi], preferred_element_type=jnp.float32))
        pooled = (jnp.concatenate(ys, axis=1) * ps_ref[layer:layer + 1, :]).astype(jnp.bfloat16)
        y_b = jnp.dot(pooled, wb_ref[...], preferred_element_type=jnp.float32)
        gates = _sigmoid(g_pre)
        merged = (gates[:, :D] * y_a + gates[:, D:] * y_b).astype(jnp.bfloat16)
        o_ref[0, r0:r0 + R, :] = x_ref[0, r0:r0 + R, :] + jnp.dot(merged, wo_ref[...],
                                                                    preferred_element_type=jnp.float32)

    n_sub = T // R
    pending = branch_in(0)
    window_sums()
    for j in range(n_sub):
        nxt = branch_in((j + 1) * R) if j + 1 < n_sub else None
        branch_out(j * R, *pending)
        pending = nxt
    uext_ref[0:HALO, :] = uext_ref[T:T + HALO, :]


def _merge(x, a_lo, a_hi, u, g, wg, wa, wp, ps, wb, wo, layer, casts=()):
    B, S, D = x.shape
    A = a_lo.shape[2]
    assert a_lo.shape[3] == SUB_TILE
    P = u.shape[2]
    T = min(ROW_TILE, S)
    assert S % T == 0 and T % SUB_TILE == 0
    row = lambda b, s: (b, s, 0)
    n_lo = S // T // 2
    assert S % (2 * T) == 0
    grid = (B, S // T)
    cast_in_specs, cast_out_specs, cast_shapes = _cast_specs(casts, grid)
    return pl.pallas_call(
        functools.partial(_merge_kernel, layer=layer, n_lo=n_lo, n_cast=len(casts)),
        grid=grid,
        in_specs=[
            pl.BlockSpec((1, T, D), row),
            pl.BlockSpec((1, T // SUB_TILE, A, SUB_TILE), lambda b, s: (b, jnp.minimum(s, n_lo - 1), 0, 0)),
            pl.BlockSpec((1, T // SUB_TILE, A, SUB_TILE), lambda b, s: (b, jnp.maximum(s - n_lo, 0), 0, 0)),
            pl.BlockSpec((1, T, P), row),
            _whole(g.shape),
            _layer_operand((D, 2 * D), layer),
            _layer_operand(wa.shape[1:], layer),
            _layer_operand(wp.shape[1:], layer),
            _whole(ps.shape),
            _layer_operand(wb.shape[1:], layer),
            _layer_operand(wo.shape[1:], layer),
            *cast_in_specs,
        ],
        out_specs=[pl.BlockSpec((1, T, D), row), *cast_out_specs],
        out_shape=[jax.ShapeDtypeStruct((B, S, D), x.dtype), *cast_shapes],
        scratch_shapes=[pltpu.VMEM((POOL_HALO + T, P), jnp.float32)] * 4,
        compiler_params=pltpu.CompilerParams(
            dimension_semantics=("parallel", "arbitrary"), vmem_limit_bytes=V7X_VMEM_LIMIT),
        name="merge",
    )(x, a_lo, a_hi, u, g, wg, wa, wp, ps, wb, wo, *(c[0] for c in casts))


def _ffn_kernel(x_ref, g_ref, wup_ref, cw_ref, cb_ref, wdn_ref, gf_ref, o_ref,
                ext_ref, carry_ref, act_ref, *, layer, final_norm):
    T = x_ref.shape[1]
    F = wdn_ref.shape[0]
    FC = FF_CHUNK
    H = CONV_HALO
    s_idx = pl.program_id(1)

    @pl.when(s_idx == 0)
    def _():
        carry_ref[...] = jnp.zeros(carry_ref.shape, jnp.float32)

    h = _rms_norm(x_ref[0], g_ref[layer:layer + 1, :]).astype(jnp.bfloat16)

    def conv_cols(c0, scale):
        a = jnp.dot(h, wup_ref[:, c0:c0 + FC], preferred_element_type=jnp.float32)
        ext_ref[0:H, :] = carry_ref[:, c0:c0 + FC]
        ext_ref[H:H + T, :] = a
        carry_ref[:, c0:c0 + FC] = a[T - H:, :]
        w = cw_ref[:, c0:c0 + FC] * scale
        return (cb_ref[layer:layer + 1, c0:c0 + FC] * scale + w[2:3, :] * a
                + w[1:2, :] * ext_ref[H - 1:H - 1 + T, :]
                + w[0:1, :] * ext_ref[H - 2:H - 2 + T, :])

    for c in range(F // FC):
        half_gate = conv_cols(c * FC, 0.5)
        val = conv_cols(F + c * FC, 1.0)
        act_ref[:, c * FC:(c + 1) * FC] = (half_gate * (jnp.tanh(half_gate) + 1.0) * val).astype(act_ref.dtype)

    y = x_ref[0] + jnp.dot(act_ref[...], wdn_ref[...], preferred_element_type=jnp.float32)
    if final_norm:
        y = _rms_norm(y, gf_ref[...])
    o_ref[0] = y


def _ffn(x, g, wup, cw, cb, wdn, gf, layer, final_norm):
    B, S, D = x.shape
    F = wdn.shape[1]
    T = min(FFN_ROW_TILE, S)
    assert S % T == 0
    row = lambda b, s: (b, s, 0)
    return pl.pallas_call(
        functools.partial(_ffn_kernel, layer=layer, final_norm=final_norm),
        grid=(B, S // T),
        in_specs=[
            pl.BlockSpec((1, T, D), row),
            _whole(g.shape),
            _layer_operand(wup.shape[1:], layer),
            _layer_operand(cw.shape[1:], layer),
            _whole(cb.shape),
            _layer_operand(wdn.shape[1:], layer),
            _whole(gf.shape),
        ],
        out_specs=pl.BlockSpec((1, T, D), row),
        out_shape=jax.ShapeDtypeStruct((B, S, D), x.dtype),
        scratch_shapes=[
            pltpu.VMEM((CONV_HALO + T, FF_CHUNK), jnp.float32),
            pltpu.VMEM((CONV_HALO, 2 * F), jnp.float32),
            pltpu.VMEM((T, F), jnp.bfloat16),
        ],
        compiler_params=pltpu.CompilerParams(
            dimension_semantics=("parallel", "arbitrary"), vmem_limit_bytes=V7X_VMEM_LIMIT),
        name="ffn",
    )(x, g, wup, cw, cb, wdn, gf)


def kernel(x, norm_mix_g, w_in, w_pool, pool_scale, w_branch_a, w_branch_b, w_out, norm_ffn_g, w_up, conv_w, conv_b, w_down, norm_final_g):
    depth = w_in.shape[0]
    D = x.shape[-1]
    A = w_branch_a.shape[1]
    P = w_branch_b.shape[1]
    assert x.shape[1] % MOBA_BLOCK == 0 and A % (2 * HEAD_DIM) == 0
    assert w_down.shape[1] % FF_CHUNK == 0 and 2 * POOL_WINDOWS[-1] <= POOL_HALO and POOL_WINDOWS == (2, 4, 8, 16) and CONV_WIDTH - 1 <= CONV_HALO
    assert 3 * A + P == 2 * D, "the branch gates are column block 1 (width 2D) of w_in"
    whole = lambda w: (w.reshape(-1, w.shape[-1]), w.shape[-1], 0)
    early = (whole(w_branch_a), whole(w_pool), whole(w_branch_b), whole(w_out),
             (w_in.reshape(-1, w_in.shape[-1]), 2 * D, 1))
    late = (whole(w_up), whole(w_down))
    gf = norm_final_g.reshape(1, D)
    for layer in range(depth):
        q, k, vt, u, *cast = _in_proj(x, norm_mix_g, w_in, layer, A, P, casts=early if layer == 0 else ())
        if layer == 0:
            wa, wp, wb, wo = (c.reshape(w.shape) for c, w in zip(cast, (w_branch_a, w_pool, w_branch_b, w_out)))
            wg = cast[4].reshape(depth, D, 2 * D)
        a_lo, a_hi = _moba(q, k, vt)
        x, *cast = _merge(x, a_lo, a_hi, u, norm_mix_g, wg, wa, wp, pool_scale, wb, wo, layer,
                          casts=late if layer == 0 else ())
        if layer == 0:
            wup, wdn = (c.reshape(w.shape) for c, w in zip(cast, (w_up, w_down)))
        x = _ffn(x, norm_ffn_g, wup, conv_w, conv_b, wdn, gf, layer, layer == depth - 1)
    return x
```

```python
import functools

import jax
import jax.numpy as jnp
from jax import lax
from jax.experimental import pallas as pl
from jax.experimental.pallas import tpu as pltpu

HEAD_DIM = 64
MOBA_BLOCK = 256
MOBA_TOPK = 3
POOL_WINDOWS = (2, 4, 8, 16)
CONV_WIDTH = 3
RMS_EPS = 1e-6
NEG_INF = -1e30
LOG2_E = 1.4426950408889634

ROW_TILE = 1024
FFN_ROW_TILE = 512
SUB_TILE = 256
POOL_HALO = 32
ONES_ROWS = 16
CONV_HALO = 8
FF_CHUNK = 256
BF16_TILE_ROWS = 16
V7X_VMEM_LIMIT = 56 * 1024 * 1024

_NT = (((1,), (1,)), ((), ()))


def _layer_operand(block, layer, index=None):
    index = (0,) * len(block) if index is None else index
    return pl.BlockSpec((None,) + tuple(block), lambda b, s: (layer,) + tuple(index),
                        pipeline_mode=pl.Buffered(1))


def _whole(shape):
    return pl.BlockSpec(shape, lambda b, s: (0,) * len(shape), pipeline_mode=pl.Buffered(1))


def _cast_specs(arrays, grid):
    n_steps = grid[0] * grid[1]
    in_specs, out_specs, shapes = [], [], []
    for arr, cols, col_block in arrays:
        rows = arr.shape[0]
        assert rows % (BF16_TILE_ROWS * n_steps) == 0
        in_specs.append(pl.BlockSpec((rows // n_steps, cols), lambda b, s, j=col_block: (b * grid[1] + s, j)))
        out_specs.append(pl.BlockSpec((rows // n_steps, cols), lambda b, s: (b * grid[1] + s, 0)))
        shapes.append(jax.ShapeDtypeStruct((rows, cols), jnp.bfloat16))
    return in_specs, out_specs, shapes


def _cast_blocks(in_refs, out_refs):
    for src, dst in zip(in_refs, out_refs, strict=True):
        dst[...] = src[...].astype(dst.dtype)


def _rms_norm(x, g):
    y = x * lax.rsqrt(jnp.mean(x * x, axis=-1, keepdims=True) + RMS_EPS)
    return y * g


def _in_proj_kernel(x_ref, g_ref, wq32_ref, wk32_ref, wv32_ref, wu32_ref, *refs, scale, layer, n_cast):
    cast_in, (q_ref, k_ref, vt_ref, u_ref) = refs[:n_cast], refs[n_cast:n_cast + 4]
    cast_out, (wq_ref, wk_ref, wvt_ref, wu_ref) = refs[n_cast + 4:2 * n_cast + 4], refs[2 * n_cast + 4:]
    _cast_blocks(cast_in, cast_out)

    @pl.when((pl.program_id(0) == 0) & (pl.program_id(1) == 0))
    def _():
        _cast_blocks((wq32_ref, wk32_ref, wu32_ref), (wq_ref, wk_ref, wu_ref))
        wvt_ref[...] = wv32_ref[...].T.astype(wvt_ref.dtype)

    for j in range(vt_ref.shape[1]):
        rows = slice(j * MOBA_BLOCK, (j + 1) * MOBA_BLOCK)
        h = _rms_norm(x_ref[0, rows, :], g_ref[layer:layer + 1, :]).astype(jnp.bfloat16)
        q_ref[0, rows, :] = (jnp.dot(h, wq_ref[...], preferred_element_type=jnp.float32) * scale).astype(q_ref.dtype)
        k_ref[0, rows, :] = jnp.dot(h, wk_ref[...], preferred_element_type=jnp.float32).astype(k_ref.dtype)
        u_ref[0, rows, :] = jnp.dot(h, wu_ref[...], preferred_element_type=jnp.float32).astype(u_ref.dtype)
        vt_ref[0, j] = lax.dot_general(wvt_ref[...], h, _NT,
                                       preferred_element_type=jnp.float32).astype(vt_ref.dtype)


def _in_proj(x, g, w_in, layer, A, P, casts=()):
    B, S, D = x.shape
    assert P == A, "u is addressed as column block 3 of width A"
    T = min(ROW_TILE, S)
    assert S % T == 0 and T % SUB_TILE == 0
    nb_t = T // MOBA_BLOCK
    nb = S // MOBA_BLOCK
    grid = (B, S // T)
    cast_in_specs, cast_out_specs, cast_shapes = _cast_specs(casts, grid)
    return pl.pallas_call(
        functools.partial(_in_proj_kernel, scale=HEAD_DIM ** -0.5 * LOG2_E, layer=layer, n_cast=len(casts)),
        grid=grid,
        in_specs=[
            pl.BlockSpec((1, T, D), lambda b, s: (b, s, 0)),
            _whole(g.shape),
            _layer_operand((D, A), layer, (0, 0)),
            _layer_operand((D, A), layer, (0, 1)),
            _layer_operand((D, A), layer, (0, 2)),
            _layer_operand((D, P), layer, (0, 3)),
            *cast_in_specs,
        ],
        out_specs=[
            pl.BlockSpec((1, T, A), lambda b, s: (b, s, 0)),
            pl.BlockSpec((1, T, A), lambda b, s: (b, s, 0)),
            pl.BlockSpec((1, nb_t, A, MOBA_BLOCK), lambda b, s: (b, s, 0, 0)),
            pl.BlockSpec((1, T, P), lambda b, s: (b, s, 0)),
            *cast_out_specs,
        ],
        out_shape=[
            jax.ShapeDtypeStruct((B, S, A), jnp.bfloat16),
            jax.ShapeDtypeStruct((B, S, A), jnp.bfloat16),
            jax.ShapeDtypeStruct((B, nb, A, MOBA_BLOCK), jnp.bfloat16),
            jax.ShapeDtypeStruct((B, S, P), jnp.bfloat16),
            *cast_shapes,
        ],
        scratch_shapes=[pltpu.VMEM((D, A), jnp.bfloat16), pltpu.VMEM((D, A), jnp.bfloat16),
                        pltpu.VMEM((A, D), jnp.bfloat16), pltpu.VMEM((D, P), jnp.bfloat16)],
        compiler_params=pltpu.CompilerParams(
            dimension_semantics=("arbitrary", "arbitrary"), vmem_limit_bytes=V7X_VMEM_LIMIT),
        name="in_proj",
    )(x, g, w_in, w_in, w_in, w_in, *(c[0] for c in casts))


def _moba_kernel(qa_ref, qb_ref, k_ref, vt_ref, oa_ref, ob_ref, kmean_ref, qza_ref, qzb_ref,
                 s0_ref, mb0_ref, s1_ref, mb1_ref, *, n_heads, n_blocks):
    L = MOBA_BLOCK
    G = 2 * HEAD_DIM
    pair = pl.program_id(1)

    @pl.when(pair == 0)
    def _():
        row = lax.broadcasted_iota(jnp.int32, (n_blocks, n_blocks * L), 0)
        col = lax.broadcasted_iota(jnp.int32, (n_blocks, n_blocks * L), 1)
        ind = jnp.where((col >= row * L) & (col < (row + 1) * L), 1.0 / L, 0.0).astype(jnp.bfloat16)
        k_all = k_ref[0].reshape(n_blocks * L, k_ref.shape[-1])
        kmean_ref[...] = jnp.dot(ind, k_all, preferred_element_type=jnp.float32)

    blk = lax.broadcasted_iota(jnp.int32, (n_blocks, L), 0)
    key_pos = lax.broadcasted_iota(jnp.int32, (L, L), 0)
    qry_pos = lax.broadcasted_iota(jnp.int32, (L, L), 1)
    lane = lax.broadcasted_iota(jnp.int32, (L, G), 1)
    ones_rows = jnp.ones((ONES_ROWS, L), jnp.bfloat16)
    groups = [slice((h // 2) * G, (h // 2 + 1) * G) for h in range(n_heads)]
    slots = ((s0_ref, mb0_ref), (s1_ref, mb1_ref))

    def pv(b, h, p):
        lhs = jnp.concatenate([vt_ref[0, b, h * HEAD_DIM:(h + 1) * HEAD_DIM, :], ones_rows], axis=0)
        return jnp.dot(lhs, p.astype(jnp.bfloat16), preferred_element_type=jnp.float32)

    def stage_a(qz_ref, b, h, slot, causal):
        s_ref, mb_ref = slot
        s = lax.dot_general(k_ref[0, b, :, groups[h]], qz_ref[h], _NT, preferred_element_type=jnp.float32)
        if causal:
            s = jnp.where(key_pos <= qry_pos, s, NEG_INF)
        s_ref[h] = s
        mb_ref[h:h + 1, :] = jnp.max(s, axis=0, keepdims=True)

    def stage_b(b, h, slot, state, on):
        s_ref, mb_ref = slot
        mb = mb_ref[h:h + 1, :]
        if state is None:
            return mb, pv(b, h, jnp.exp2(s_ref[h] - mb))
        m, acc = state
        if on is None:
            m_new = jnp.maximum(m, mb)
            shift = m_new
        else:
            m_new = jnp.where(on, jnp.maximum(m, mb), m)
            shift = jnp.where(on, m_new, -NEG_INF)
        return m_new, jnp.exp2(m - m_new) * acc + pv(b, h, jnp.exp2(s_ref[h] - shift))

    def prepare(ti, q_ref, qz_ref):
        sel = []
        for h in range(n_heads):
            in_head = (lane >= (h % 2) * HEAD_DIM) & (lane < (h % 2 + 1) * HEAD_DIM)
            qz_ref[h] = jnp.where(in_head, q_ref[0, :, groups[h]], jnp.zeros((L, G), q_ref.dtype))
            if ti <= MOBA_TOPK:
                sel.append(None)
                continue
            km = kmean_ref[:, groups[h]]
            km_hi = km.astype(jnp.bfloat16)
            km_lo = (km - km_hi.astype(jnp.float32)).astype(jnp.bfloat16)
            g2 = lax.dot_general(jnp.concatenate([km_hi, km_lo], axis=0), qz_ref[h], _NT,
                                 preferred_element_type=jnp.float32)
            gate = g2[:n_blocks] + g2[n_blocks:]
            rank = jnp.zeros((n_blocks, L), jnp.float32)
            for j in range(ti):
                gj = gate[j:j + 1, :]
                rank = rank + jnp.where((gj > gate) | ((gj == gate) & (blk > j)), 1.0, 0.0)
            sel.append(jnp.where(rank < MOBA_TOPK, 1.0, 0.0))
        return sel

    def finalize(state, o_ref):
        for h in range(n_heads):
            acc = state[h][1]
            inv_l = 1.0 / acc[HEAD_DIM:HEAD_DIM + 1, :]
            o_ref[0, 0, h * HEAD_DIM:(h + 1) * HEAD_DIM, :] = (acc[0:HEAD_DIM, :] * inv_l).astype(o_ref.dtype)

    def tile_pair(tp):
        tiles = ((tp, qa_ref, qza_ref, oa_ref), (n_blocks - 1 - tp, qb_ref, qzb_ref, ob_ref))
        sels = [prepare(ti, q_ref, qz_ref) for ti, q_ref, qz_ref, _ in tiles]
        visits = [(w, b, n == 0, n == tiles[w][0]) for w in range(2)
                  for n, b in enumerate([tiles[w][0]] + list(range(tiles[w][0])))]
        states = [[None] * n_heads, [None] * n_heads]
        for h in range(n_heads):
            stage_a(tiles[0][2], visits[0][1], h, slots[0], causal=True)
        for n, (w, b, first, last) in enumerate(visits):
            for h in range(n_heads):
                if n + 1 < len(visits):
                    w2, b2, first2, _ = visits[n + 1]
                    stage_a(tiles[w2][2], b2, h, slots[(n + 1) % 2], causal=first2)
                on = None if (first or sels[w][h] is None) else sels[w][h][b:b + 1, :] > 0.0
                states[w][h] = stage_b(b, h, slots[n % 2], states[w][h], on)
            if last:
                finalize(states[w], tiles[w][3])

    for tp in range(n_blocks // 2):
        pl.when(pair == tp)(functools.partial(tile_pair, tp))


def _moba(q, k, vt):
    B, S, A = q.shape
    L = MOBA_BLOCK
    nb = S // L
    H = A // HEAD_DIM
    assert nb % 2 == 0
    k4 = k.reshape(B, nb, L, A)
    half = jax.ShapeDtypeStruct((B, nb // 2, A, L), jnp.bfloat16)
    return pl.pallas_call(
        functools.partial(_moba_kernel, n_heads=H, n_blocks=nb),
        grid=(B, nb // 2),
        in_specs=[
            pl.BlockSpec((1, L, A), lambda b, j: (b, j, 0)),
            pl.BlockSpec((1, L, A), lambda b, j: (b, nb - 1 - j, 0)),
            pl.BlockSpec((1, nb, L, A), lambda b, j: (b, 0, 0, 0)),
            pl.BlockSpec((1, nb, A, L), lambda b, j: (b, 0, 0, 0)),
        ],
        out_specs=[
            pl.BlockSpec((1, 1, A, L), lambda b, j: (b, j, 0, 0)),
            pl.BlockSpec((1, 1, A, L), lambda b, j: (b, nb // 2 - 1 - j, 0, 0)),
        ],
        out_shape=[half, half],
        scratch_shapes=[
            pltpu.VMEM((nb, A), jnp.float32),
            pltpu.VMEM((H, L, 2 * HEAD_DIM), jnp.bfloat16),
            pltpu.VMEM((H, L, 2 * HEAD_DIM), jnp.bfloat16),
            pltpu.VMEM((H, L, L), jnp.float32),
            pltpu.VMEM((H, L), jnp.float32),
            pltpu.VMEM((H, L, L), jnp.float32),
            pltpu.VMEM((H, L), jnp.float32),
        ],
        compiler_params=pltpu.CompilerParams(
            dimension_semantics=("parallel", "arbitrary"), vmem_limit_bytes=V7X_VMEM_LIMIT),
        name="moba",
    )(q, q, k4, vt)


def _merge_kernel(x_ref, a_lo_ref, a_hi_ref, u_ref, g_ref, wg_ref, wa_ref, wp_ref, ps_ref, wb_ref, wo_ref,
                  *refs, layer, n_lo, n_cast):
    cast_in, o_ref, cast_out = refs[:n_cast], refs[n_cast], refs[n_cast + 1:2 * n_cast + 1]
    uext_ref, w2_ref, w4_ref, w8_ref = refs[2 * n_cast + 1:]
    _cast_blocks(cast_in, cast_out)
    T = x_ref.shape[1]
    D = x_ref.shape[2]
    GW = wp_ref.shape[1]
    R = SUB_TILE
    HALO = POOL_HALO
    s_idx = pl.program_id(1)

    @pl.when(s_idx == 0)
    def _():
        uext_ref[0:HALO, :] = jnp.zeros((HALO, uext_ref.shape[1]), jnp.float32)

    def branch_in(r0):
        at = jnp.where(s_idx < n_lo, a_lo_ref[0, r0 // R], a_hi_ref[0, r0 // R])
        y_a = lax.dot_general(at, wa_ref[...], (((0,), (0,)), ((), ())), preferred_element_type=jnp.float32)
        h_half = _rms_norm(x_ref[0, r0:r0 + R, :], 0.5 * g_ref[layer:layer + 1, :]).astype(jnp.bfloat16)
        return y_a, jnp.dot(h_half, wg_ref[...], preferred_element_type=jnp.float32)

    def window_sums():
        n = HALO + T
        uext_ref[HALO:n, :] = u_ref[0].astype(jnp.float32)
        w2_ref[8:n, :] = uext_ref[8:n, :] + uext_ref[7:n - 1, :]
        w4_ref[16:n, GW:] = w2_ref[16:n, GW:] + w2_ref[14:n - 2, GW:]
        w8_ref[24:n, 2 * GW:] = w4_ref[24:n, 2 * GW:] + w4_ref[20:n - 4, 2 * GW:]

    def branch_out(r0, y_a, g_half):
        e0 = HALO + r0
        wsums = (w2_ref[e0:e0 + R, 0:GW], w4_ref[e0:e0 + R, GW:2 * GW], w8_ref[e0:e0 + R, 2 * GW:3 * GW],
                 w8_ref[e0:e0 + R, 3 * GW:] + w8_ref[e0 - 8:e0 - 8 + R, 3 * GW:])
        t_pos = s_idx * T + r0 + lax.broadcasted_iota(jnp.int32, (R, GW), 0)
        ys = []
        for gi, win in enumerate(POOL_WINDOWS):
            if r0 >= win:
                mean = wsums[gi] * (1.0 / win)
            else:
                mean = wsums[gi] / jnp.minimum(t_pos + 1, win).astype(jnp.float32)
            mixed = (mean - uext_ref[e0:e0 + R, gi * GW:(gi + 1) * GW]).astype(jnp.bfloat16)
            ys.append(jnp.dot(mixed, wp_ref[gi], preferred_element_type=jnp.float32))
        pooled = (jnp.concatenate(ys, axis=1) * ps_ref[layer:layer + 1, :]).astype(jnp.bfloat16)
        y_b = jnp.dot(pooled, wb_ref[...], preferred_element_type=jnp.float32)
        t = jnp.tanh(g_half)
        merged = (0.5 * ((t[:, :D] + 1.0) * y_a + (t[:, D:] + 1.0) * y_b)).astype(jnp.bfloat16)
        o_ref[0, r0:r0 + R, :] = x_ref[0, r0:r0 + R, :] + jnp.dot(merged, wo_ref[...],
                                                                    preferred_element_type=jnp.float32)

    n_sub = T // R
    pending = branch_in(0)
    window_sums()
    for j in range(n_sub):
        nxt = branch_in((j + 1) * R) if j + 1 < n_sub else None
        branch_out(j * R, *pending)
        pending = nxt
    uext_ref[0:HALO, :] = uext_ref[T:T + HALO, :]


def _merge(x, a_lo, a_hi, u, g, wg, wa, wp, ps, wb, wo, layer, casts=()):
    B, S, D = x.shape
    A = a_lo.shape[2]
    assert a_lo.shape[3] == SUB_TILE
    P = u.shape[2]
    T = min(ROW_TILE, S)
    assert S % T == 0 and T % SUB_TILE == 0
    row = lambda b, s: (b, s, 0)
    n_lo = S // T // 2
    assert S % (2 * T) == 0
    grid = (B, S // T)
    cast_in_specs, cast_out_specs, cast_shapes = _cast_specs(casts, grid)
    return pl.pallas_call(
        functools.partial(_merge_kernel, layer=layer, n_lo=n_lo, n_cast=len(casts)),
        grid=grid,
        in_specs=[
            pl.BlockSpec((1, T, D), row),
            pl.BlockSpec((1, T // SUB_TILE, A, SUB_TILE), lambda b, s: (b, jnp.minimum(s, n_lo - 1), 0, 0)),
            pl.BlockSpec((1, T // SUB_TILE, A, SUB_TILE), lambda b, s: (b, jnp.maximum(s - n_lo, 0), 0, 0)),
            pl.BlockSpec((1, T, P), row),
            _whole(g.shape),
            _layer_operand((D, 2 * D), layer),
            _layer_operand(wa.shape[1:], layer),
            _layer_operand(wp.shape[1:], layer),
            _whole(ps.shape),
            _layer_operand(wb.shape[1:], layer),
            _layer_operand(wo.shape[1:], layer),
            *cast_in_specs,
        ],
        out_specs=[pl.BlockSpec((1, T, D), row), *cast_out_specs],
        out_shape=[jax.ShapeDtypeStruct((B, S, D), x.dtype), *cast_shapes],
        scratch_shapes=[pltpu.VMEM((POOL_HALO + T, P), jnp.float32)] * 4,
        compiler_params=pltpu.CompilerParams(
            dimension_semantics=("parallel", "arbitrary"), vmem_limit_bytes=V7X_VMEM_LIMIT),
        name="merge",
    )(x, a_lo, a_hi, u, g, wg, wa, wp, ps, wb, wo, *(c[0] for c in casts))


def _ffn_kernel(x_ref, g_ref, wup_ref, cw_ref, cb_ref, wdn_ref, gf_ref, o_ref,
                ext_ref, carry_ref, act_ref, *, layer, final_norm):
    T = x_ref.shape[1]
    F = wdn_ref.shape[0]
    FC = FF_CHUNK
    H = CONV_HALO
    s_idx = pl.program_id(1)

    @pl.when(s_idx == 0)
    def _():
        carry_ref[...] = jnp.zeros(carry_ref.shape, jnp.float32)

    h = _rms_norm(x_ref[0], g_ref[layer:layer + 1, :]).astype(jnp.bfloat16)

    def conv_cols(c0, scale):
        a = jnp.dot(h, wup_ref[:, c0:c0 + FC], preferred_element_type=jnp.float32)
        ext_ref[0:H, :] = carry_ref[:, c0:c0 + FC]
        ext_ref[H:H + T, :] = a
        carry_ref[:, c0:c0 + FC] = a[T - H:, :]
        w = cw_ref[:, c0:c0 + FC] * scale
        return (cb_ref[layer:layer + 1, c0:c0 + FC] * scale + w[2:3, :] * a
                + w[1:2, :] * ext_ref[H - 1:H - 1 + T, :]
                + w[0:1, :] * ext_ref[H - 2:H - 2 + T, :])

    for c in range(F // FC):
        half_gate = conv_cols(c * FC, 0.5)
        val = conv_cols(F + c * FC, 1.0)
        act_ref[:, c * FC:(c + 1) * FC] = (half_gate * (jnp.tanh(half_gate) + 1.0) * val).astype(act_ref.dtype)

    y = x_ref[0] + jnp.dot(act_ref[...], wdn_ref[...], preferred_element_type=jnp.float32)
    if final_norm:
        y = _rms_norm(y, gf_ref[...])
    o_ref[0] = y


def _ffn(x, g, wup, cw, cb, wdn, gf, layer, final_norm):
    B, S, D = x.shape
    F = wdn.shape[1]
    T = min(FFN_ROW_TILE, S)
    assert S % T == 0
    row = lambda b, s: (b, s, 0)
    return pl.pallas_call(
        functools.partial(_ffn_kernel, layer=layer, final_norm=final_norm),
        grid=(B, S // T),
        in_specs=[
            pl.BlockSpec((1, T, D), row),
            _whole(g.shape),
            _layer_operand(wup.shape[1:], layer),
            _layer_operand(cw.shape[1:], layer),
            _whole(cb.shape),
            _layer_operand(wdn.shape[1:], layer),
            _whole(gf.shape),
        ],
        out_specs=pl.BlockSpec((1, T, D), row),
        out_shape=jax.ShapeDtypeStruct((B, S, D), x.dtype),
        scratch_shapes=[
            pltpu.VMEM((CONV_HALO + T, FF_CHUNK), jnp.float32),
            pltpu.VMEM((CONV_HALO, 2 * F), jnp.float32),
            pltpu.VMEM((T, F), jnp.bfloat16),
        ],
        compiler_params=pltpu.CompilerParams(
            dimension_semantics=("parallel", "arbitrary"), vmem_limit_bytes=V7X_VMEM_LIMIT),
        name="ffn",
    )(x, g, wup, cw, cb, wdn, gf)


def kernel(x, norm_mix_g, w_in, w_pool, pool_scale, w_branch_a, w_branch_b, w_out, norm_ffn_g, w_up, conv_w, conv_b, w_down, norm_final_g):
    depth = w_in.shape[0]
    D = x.shape[-1]
    A = w_branch_a.shape[1]
    P = w_branch_b.shape[1]
    assert x.shape[1] % MOBA_BLOCK == 0 and A % (2 * HEAD_DIM) == 0
    assert w_down.shape[1] % FF_CHUNK == 0 and 2 * POOL_WINDOWS[-1] <= POOL_HALO and POOL_WINDOWS == (2, 4, 8, 16) and CONV_WIDTH - 1 <= CONV_HALO
    assert 3 * A + P == 2 * D, "the branch gates are column block 1 (width 2D) of w_in"
    whole = lambda w: (w.reshape(-1, w.shape[-1]), w.shape[-1], 0)
    early = (whole(w_branch_a), whole(w_pool), whole(w_branch_b), whole(w_out),
             (w_in.reshape(-1, w_in.shape[-1]), 2 * D, 1))
    late = (whole(w_up), whole(w_down))
    gf = norm_final_g.reshape(1, D)
    for layer in range(depth):
        q, k, vt, u, *cast = _in_proj(x, norm_mix_g, w_in, layer, A, P, casts=early if layer == 0 else ())
        if layer == 0:
            wa, wp, wb, wo = (c.reshape(w.shape) for c, w in zip(cast, (w_branch_a, w_pool, w_branch_b, w_out)))
            wg = cast[4].reshape(depth, D, 2 * D)
        a_lo, a_hi = _moba(q, k, vt)
        x, *cast = _merge(x, a_lo, a_hi, u, norm_mix_g, wg, wa, wp, pool_scale, wb, wo, layer,
                          casts=late if layer == 0 else ())
        if layer == 0:
            wup, wdn = (c.reshape(w.shape) for c, w in zip(cast, (w_up, w_down)))
        x = _ffn(x, norm_ffn_g, wup, conv_w, conv_b, wdn, gf, layer, layer == depth - 1)
    return x
```

```python
import functools

import jax
import jax.numpy as jnp
from jax import lax
from jax.experimental import pallas as pl
from jax.experimental.pallas import tpu as pltpu

HEAD_DIM = 64
MOBA_BLOCK = 256
MOBA_TOPK = 3
POOL_WINDOWS = (2, 4, 8, 16)
CONV_WIDTH = 3
RMS_EPS = 1e-6
NEG_INF = -1e30
LOG2_E = 1.4426950408889634

ROW_TILE = 1024
FFN_ROW_TILE = 512
SUB_TILE = 256
POOL_HALO = 32
ONES_ROWS = 16
CONV_HALO = 8
FF_CHUNK = 256
BF16_TILE_ROWS = 16
V7X_VMEM_LIMIT = 56 * 1024 * 1024

_NT = (((1,), (1,)), ((), ()))


def _layer_operand(block, layer, index=None):
    index = (0,) * len(block) if index is None else index
    return pl.BlockSpec((None,) + tuple(block), lambda b, s: (layer,) + tuple(index),
                        pipeline_mode=pl.Buffered(1))


def _whole(shape):
    return pl.BlockSpec(shape, lambda b, s: (0,) * len(shape), pipeline_mode=pl.Buffered(1))


def _cast_specs(arrays, grid):
    n_steps = grid[0] * grid[1]
    in_specs, out_specs, shapes = [], [], []
    for arr, cols, col_block in arrays:
        rows = arr.shape[0]
        assert rows % (BF16_TILE_ROWS * n_steps) == 0
        in_specs.append(pl.BlockSpec((rows // n_steps, cols), lambda b, s, j=col_block: (b * grid[1] + s, j)))
        out_specs.append(pl.BlockSpec((rows // n_steps, cols), lambda b, s: (b * grid[1] + s, 0)))
        shapes.append(jax.ShapeDtypeStruct((rows, cols), jnp.bfloat16))
    return in_specs, out_specs, shapes


def _cast_blocks(in_refs, out_refs):
    for src, dst in zip(in_refs, out_refs, strict=True):
        dst[...] = src[...].astype(dst.dtype)


def _rms_norm(x, g):
    y = x * lax.rsqrt(jnp.mean(x * x, axis=-1, keepdims=True) + RMS_EPS)
    return y * g


def _in_proj_kernel(x_ref, g_ref, wq32_ref, wk32_ref, wv32_ref, wu32_ref, *refs, scale, layer, n_cast):
    cast_in, (q_ref, k_ref, vt_ref, u_ref) = refs[:n_cast], refs[n_cast:n_cast + 4]
    cast_out, (wq_ref, wk_ref, wvt_ref, wu_ref) = refs[n_cast + 4:2 * n_cast + 4], refs[2 * n_cast + 4:]
    _cast_blocks(cast_in, cast_out)

    @pl.when((pl.program_id(0) == 0) & (pl.program_id(1) == 0))
    def _():
        _cast_blocks((wq32_ref, wk32_ref, wu32_ref), (wq_ref, wk_ref, wu_ref))
        wvt_ref[...] = wv32_ref[...].T.astype(wvt_ref.dtype)

    for j in range(vt_ref.shape[1]):
        rows = slice(j * MOBA_BLOCK, (j + 1) * MOBA_BLOCK)
        h = _rms_norm(x_ref[0, rows, :], g_ref[layer:layer + 1, :]).astype(jnp.bfloat16)
        q_ref[0, rows, :] = (jnp.dot(h, wq_ref[...], preferred_element_type=jnp.float32) * scale).astype(q_ref.dtype)
        k_ref[0, rows, :] = jnp.dot(h, wk_ref[...], preferred_element_type=jnp.float32).astype(k_ref.dtype)
        u_ref[0, rows, :] = jnp.dot(h, wu_ref[...], preferred_element_type=jnp.float32).astype(u_ref.dtype)
        vt_ref[0, j] = lax.dot_general(wvt_ref[...], h, _NT,
                                       preferred_element_type=jnp.float32).astype(vt_ref.dtype)


def _in_proj(x, g, w_in, layer, A, P, casts=()):
    B, S, D = x.shape
    assert P == A, "u is addressed as column block 3 of width A"
    T = min(ROW_TILE, S)
    assert S % T == 0 and T % SUB_TILE == 0
    nb_t = T // MOBA_BLOCK
    nb = S // MOBA_BLOCK
    grid = (B, S // T)
    cast_in_specs, cast_out_specs, cast_shapes = _cast_specs(casts, grid)
    return pl.pallas_call(
        functools.partial(_in_proj_kernel, scale=HEAD_DIM ** -0.5 * LOG2_E, layer=layer, n_cast=len(casts)),
        grid=grid,
        in_specs=[
            pl.BlockSpec((1, T, D), lambda b, s: (b, s, 0)),
            _whole(g.shape),
            _layer_operand((D, A), layer, (0, 0)),
            _layer_operand((D, A), layer, (0, 1)),
            _layer_operand((D, A), layer, (0, 2)),
            _layer_operand((D, P), layer, (0, 3)),
            *cast_in_specs,
        ],
        out_specs=[
            pl.BlockSpec((1, T, A), lambda b, s: (b, s, 0)),
            pl.BlockSpec((1, T, A), lambda b, s: (b, s, 0)),
            pl.BlockSpec((1, nb_t, A, MOBA_BLOCK), lambda b, s: (b, s, 0, 0)),
            pl.BlockSpec((1, T, P), lambda b, s: (b, s, 0)),
            *cast_out_specs,
        ],
        out_shape=[
            jax.ShapeDtypeStruct((B, S, A), jnp.bfloat16),
            jax.ShapeDtypeStruct((B, S, A), jnp.bfloat16),
            jax.ShapeDtypeStruct((B, nb, A, MOBA_BLOCK), jnp.bfloat16),
            jax.ShapeDtypeStruct((B, S, P), jnp.bfloat16),
            *cast_shapes,
        ],
        scratch_shapes=[pltpu.VMEM((D, A), jnp.bfloat16), pltpu.VMEM((D, A), jnp.bfloat16),
                        pltpu.VMEM((A, D), jnp.bfloat16), pltpu.VMEM((D, P), jnp.bfloat16)],
        compiler_params=pltpu.CompilerParams(
            dimension_semantics=("arbitrary", "arbitrary"), vmem_limit_bytes=V7X_VMEM_LIMIT),
        name="in_proj",
    )(x, g, w_in, w_in, w_in, w_in, *(c[0] for c in casts))


def _moba_kernel(qa_ref, qb_ref, k_ref, vt_ref, oa_ref, ob_ref, kmean_ref, qza_ref, qzb_ref,
                 s0_ref, mb0_ref, s1_ref, mb1_ref, *, n_heads, n_blocks):
    L = MOBA_BLOCK
    G = 2 * HEAD_DIM
    pair = pl.program_id(1)

    @pl.when(pair == 0)
    def _():
        row = lax.broadcasted_iota(jnp.int32, (n_blocks, n_blocks * L), 0)
        col = lax.broadcasted_iota(jnp.int32, (n_blocks, n_blocks * L), 1)
        ind = jnp.where((col >= row * L) & (col < (row + 1) * L), 1.0 / L, 0.0).astype(jnp.bfloat16)
        k_all = k_ref[0].reshape(n_blocks * L, k_ref.shape[-1])
        kmean_ref[...] = jnp.dot(ind, k_all, preferred_element_type=jnp.float32)

    blk = lax.broadcasted_iota(jnp.int32, (n_blocks, L), 0)
    key_pos = lax.broadcasted_iota(jnp.int32, (L, L), 0)
    qry_pos = lax.broadcasted_iota(jnp.int32, (L, L), 1)
    lane = lax.broadcasted_iota(jnp.int32, (L, G), 1)
    ones_rows = jnp.ones((ONES_ROWS, L), jnp.bfloat16)
    groups = [slice((h // 2) * G, (h // 2 + 1) * G) for h in range(n_heads)]
    slots = ((s0_ref, mb0_ref), (s1_ref, mb1_ref))

    def pv(b, h, p):
        lhs = jnp.concatenate([vt_ref[0, b, h * HEAD_DIM:(h + 1) * HEAD_DIM, :], ones_rows], axis=0)
        return jnp.dot(lhs, p.astype(jnp.bfloat16), preferred_element_type=jnp.float32)

    def stage_a(qz_ref, b, h, slot, causal):
        s_ref, mb_ref = slot
        s = lax.dot_general(k_ref[0, b, :, groups[h]], qz_ref[h], _NT, preferred_element_type=jnp.float32)
        if causal:
            s = jnp.where(key_pos <= qry_pos, s, NEG_INF)
        s_ref[h] = s
        mb_ref[h:h + 1, :] = jnp.max(s, axis=0, keepdims=True)

    def stage_b(b, h, slot, state, on):
        s_ref, mb_ref = slot
        mb = mb_ref[h:h + 1, :]
        if state is None:
            return mb, pv(b, h, jnp.exp2(s_ref[h] - mb))
        m, acc = state
        if on is None:
            m_new = jnp.maximum(m, mb)
            shift = m_new
        else:
            m_new = jnp.where(on, jnp.maximum(m, mb), m)
            shift = jnp.where(on, m_new, -NEG_INF)
        return m_new, jnp.exp2(m - m_new) * acc + pv(b, h, jnp.exp2(s_ref[h] - shift))

    def prepare(ti, q_ref, qz_ref):
        sel = []
        for h in range(n_heads):
            in_head = (lane >= (h % 2) * HEAD_DIM) & (lane < (h % 2 + 1) * HEAD_DIM)
            qz_ref[h] = jnp.where(in_head, q_ref[0, :, groups[h]], jnp.zeros((L, G), q_ref.dtype))
            if ti <= MOBA_TOPK:
                sel.append(None)
                continue
            km = kmean_ref[:, groups[h]]
            km_hi = km.astype(jnp.bfloat16)
            km_lo = (km - km_hi.astype(jnp.float32)).astype(jnp.bfloat16)
            g2 = lax.dot_general(jnp.concatenate([km_hi, km_lo], axis=0), qz_ref[h], _NT,
                                 preferred_element_type=jnp.float32)
            gate = g2[:n_blocks] + g2[n_blocks:]
            rank = jnp.zeros((n_blocks, L), jnp.float32)
            for j in range(ti):
                gj = gate[j:j + 1, :]
                rank = rank + jnp.where((gj > gate) | ((gj == gate) & (blk > j)), 1.0, 0.0)
            sel.append(jnp.where(rank < MOBA_TOPK, 1.0, 0.0))
        return sel

    def finalize(state, o_ref):
        for h in range(n_heads):
            acc = state[h][1]
            inv_l = 1.0 / acc[HEAD_DIM:HEAD_DIM + 1, :]
            o_ref[0, 0, h * HEAD_DIM:(h + 1) * HEAD_DIM, :] = (acc[0:HEAD_DIM, :] * inv_l).astype(o_ref.dtype)

    def tile_pair(tp):
        tiles = ((tp, qa_ref, qza_ref, oa_ref), (n_blocks - 1 - tp, qb_ref, qzb_ref, ob_ref))
        sels = [prepare(ti, q_ref, qz_ref) for ti, q_ref, qz_ref, _ in tiles]
        visits = [(w, b, n == 0, n == tiles[w][0]) for w in range(2)
                  for n, b in enumerate([tiles[w][0]] + list(range(tiles[w][0])))]
        states = [[None] * n_heads, [None] * n_heads]
        for h in range(n_heads):
            stage_a(tiles[0][2], visits[0][1], h, slots[0], causal=True)
        for n, (w, b, first, last) in enumerate(visits):
            for h in range(n_heads):
                if n + 1 < len(visits):
                    w2, b2, first2, _ = visits[n + 1]
                    stage_a(tiles[w2][2], b2, h, slots[(n + 1) % 2], causal=first2)
                on = None if (first or sels[w][h] is None) else sels[w][h][b:b + 1, :] > 0.0
                states[w][h] = stage_b(b, h, slots[n % 2], states[w][h], on)
            if last:
                finalize(states[w], tiles[w][3])

    for tp in range(n_blocks // 2):
        pl.when(pair == tp)(functools.partial(tile_pair, tp))


def _moba(q, k, vt):
    B, S, A = q.shape
    L = MOBA_BLOCK
    nb = S // L
    H = A // HEAD_DIM
    assert nb % 2 == 0
    k4 = k.reshape(B, nb, L, A)
    half = jax.ShapeDtypeStruct((B, nb // 2, A, L), jnp.bfloat16)
    return pl.pallas_call(
        functools.partial(_moba_kernel, n_heads=H, n_blocks=nb),
        grid=(B, nb // 2),
        in_specs=[
            pl.BlockSpec((1, L, A), lambda b, j: (b, j, 0)),
            pl.BlockSpec((1, L, A), lambda b, j: (b, nb - 1 - j, 0)),
            pl.BlockSpec((1, nb, L, A), lambda b, j: (b, 0, 0, 0)),
            pl.BlockSpec((1, nb, A, L), lambda b, j: (b, 0, 0, 0)),
        ],
        out_specs=[
            pl.BlockSpec((1, 1, A, L), lambda b, j: (b, j, 0, 0)),
            pl.BlockSpec((1, 1, A, L), lambda b, j: (b, nb // 2 - 1 - j, 0, 0)),
        ],
        out_shape=[half, half],
        scratch_shapes=[
            pltpu.VMEM((nb, A), jnp.float32),
            pltpu.VMEM((H, L, 2 * HEAD_DIM), jnp.bfloat16),
            pltpu.VMEM((H, L, 2 * HEAD_DIM), jnp.bfloat16),
            pltpu.VMEM((H, L, L), jnp.float32),
            pltpu.VMEM((H, L), jnp.float32),
            pltpu.VMEM((H, L, L), jnp.float32),
            pltpu.VMEM((H, L), jnp.float32),
        ],
        compiler_params=pltpu.CompilerParams(
            dimension_semantics=("parallel", "arbitrary"), vmem_limit_bytes=V7X_VMEM_LIMIT),
        name="moba",
    )(q, q, k4, vt)


def _merge_kernel(x_ref, a_lo_ref, a_hi_ref, u_ref, g_ref, wg_ref, wa_ref, wp_ref, ps_ref, wb_ref, wo_ref,
                  *refs, layer, n_lo, n_cast):
    cast_in, o_ref, cast_out = refs[:n_cast], refs[n_cast], refs[n_cast + 1:2 * n_cast + 1]
    uext_ref, w2_ref, w4_ref, w8_ref = refs[2 * n_cast + 1:]
    _cast_blocks(cast_in, cast_out)
    T = x_ref.shape[1]
    D = x_ref.shape[2]
    GW = wp_ref.shape[1]
    R = SUB_TILE
    HALO = POOL_HALO
    s_idx = pl.program_id(1)

    @pl.when(s_idx == 0)
    def _():
        uext_ref[0:HALO, :] = jnp.zeros((HALO, uext_ref.shape[1]), jnp.float32)

    def branch_in(r0):
        at = jnp.where(s_idx < n_lo, a_lo_ref[0, r0 // R], a_hi_ref[0, r0 // R])
        y_a = lax.dot_general(at, wa_ref[...], (((0,), (0,)), ((), ())), preferred_element_type=jnp.float32)
        h_half = _rms_norm(x_ref[0, r0:r0 + R, :], 0.5 * g_ref[layer:layer + 1, :]).astype(jnp.bfloat16)
        return y_a, jnp.dot(h_half, wg_ref[...], preferred_element_type=jnp.float32)

    def window_sums():
        n = HALO + T
        uext_ref[HALO:n, :] = u_ref[0].astype(jnp.float32)
        w2_ref[8:n, :] = uext_ref[8:n, :] + uext_ref[7:n - 1, :]
        w4_ref[16:n, GW:] = w2_ref[16:n, GW:] + w2_ref[14:n - 2, GW:]
        w8_ref[24:n, 2 * GW:] = w4_ref[24:n, 2 * GW:] + w4_ref[20:n - 4, 2 * GW:]

    def branch_out(r0, y_a, g_half):
        e0 = HALO + r0
        wsums = (w2_ref[e0:e0 + R, 0:GW], w4_ref[e0:e0 + R, GW:2 * GW], w8_ref[e0:e0 + R, 2 * GW:3 * GW],
                 w8_ref[e0:e0 + R, 3 * GW:] + w8_ref[e0 - 8:e0 - 8 + R, 3 * GW:])
        t_pos = s_idx * T + r0 + lax.broadcasted_iota(jnp.int32, (R, GW), 0)
        ys = []
        for gi, win in enumerate(POOL_WINDOWS):
            if r0 >= win:
                mean = wsums[gi] * (1.0 / win)
            else:
                mean = wsums[gi] / jnp.minimum(t_pos + 1, win).astype(jnp.float32)
            mixed = (mean - uext_ref[e0:e0 + R, gi * GW:(gi + 1) * GW]).astype(jnp.bfloat16)
            ys.append(jnp.dot(mixed, wp_ref[gi], preferred_element_type=jnp.float32))
        pooled = (jnp.concatenate(ys, axis=1) * ps_ref[layer:layer + 1, :]).astype(jnp.bfloat16)
        y_b = jnp.dot(pooled, wb_ref[...], preferred_element_type=jnp.float32)
        t = jnp.tanh(g_half)
        merged = (0.5 * ((t[:, :D] + 1.0) * y_a + (t[:, D:] + 1.0) * y_b)).astype(jnp.bfloat16)
        o_ref[0, r0:r0 + R, :] = x_ref[0, r0:r0 + R, :] + jnp.dot(merged, wo_ref[...],
                                                                    preferred_element_type=jnp.float32)

    n_sub = T // R
    pending = branch_in(0)
    window_sums()
    for j in range(n_sub):
        nxt = branch_in((j + 1) * R) if j + 1 < n_sub else None
        branch_out(j * R, *pending)
        pending = nxt
    uext_ref[0:HALO, :] = uext_ref[T:T + HALO, :]


def _merge(x, a_lo, a_hi, u, g, wg, wa, wp, ps, wb, wo, layer, casts=()):
    B, S, D = x.shape
    A = a_lo.shape[2]
    assert a_lo.shape[3] == SUB_TILE
    P = u.shape[2]
    T = min(ROW_TILE, S)
    assert S % T == 0 and T % SUB_TILE == 0
    row = lambda b, s: (b, s, 0)
    n_lo = S // T // 2
    assert S % (2 * T) == 0
    grid = (B, S // T)
    cast_in_specs, cast_out_specs, cast_shapes = _cast_specs(casts, grid)
    return pl.pallas_call(
        functools.partial(_merge_kernel, layer=layer, n_lo=n_lo, n_cast=len(casts)),
        grid=grid,
        in_specs=[
            pl.BlockSpec((1, T, D), row),
            pl.BlockSpec((1, T // SUB_TILE, A, SUB_TILE), lambda b, s: (b, jnp.minimum(s, n_lo - 1), 0, 0)),
            pl.BlockSpec((1, T // SUB_TILE, A, SUB_TILE), lambda b, s: (b, jnp.maximum(s - n_lo, 0), 0, 0)),
            pl.BlockSpec((1, T, P), row),
            _whole(g.shape),
            _layer_operand((D, 2 * D), layer),
            _layer_operand(wa.shape[1:], layer),
            _layer_operand(wp.shape[1:], layer),
            _whole(ps.shape),
            _layer_operand(wb.shape[1:], layer),
            _layer_operand(wo.shape[1:], layer),
            *cast_in_specs,
        ],
        out_specs=[pl.BlockSpec((1, T, D), row), *cast_out_specs],
        out_shape=[jax.ShapeDtypeStruct((B, S, D), x.dtype), *cast_shapes],
        scratch_shapes=[pltpu.VMEM((POOL_HALO + T, P), jnp.float32)] * 4,
        compiler_params=pltpu.CompilerParams(
            dimension_semantics=("parallel", "arbitrary"), vmem_limit_bytes=V7X_VMEM_LIMIT),
        name="merge",
    )(x, a_lo, a_hi, u, g, wg, wa, wp, ps, wb, wo, *(c[0] for c in casts))


def _ffn_kernel(x_ref, g_ref, wup_ref, cw_ref, cb_ref, wdn_ref, gf_ref, o_ref,
                ext_ref, carry_ref, act_ref, *, layer, final_norm):
    T = x_ref.shape[1]
    F = wdn_ref.shape[0]
    FC = FF_CHUNK
    H = CONV_HALO
    s_idx = pl.program_id(1)

    @pl.when(s_idx == 0)
    def _():
        carry_ref[...] = jnp.zeros(carry_ref.shape, jnp.float32)

    h = _rms_norm(x_ref[0], g_ref[layer:layer + 1, :]).astype(jnp.bfloat16)

    def conv_cols(c0, scale):
        ext_ref[0:H, :] = carry_ref[:, c0:c0 + FC]
        ext_ref[H:H + T, :] = jnp.dot(h, wup_ref[:, c0:c0 + FC], preferred_element_type=jnp.float32)
        carry_ref[:, c0:c0 + FC] = ext_ref[T:T + H, :]
        w = cw_ref[:, c0:c0 + FC] * scale
        return (cb_ref[layer:layer + 1, c0:c0 + FC] * scale + w[2:3, :] * ext_ref[H:H + T, :]
                + w[1:2, :] * ext_ref[H - 1:H - 1 + T, :]
                + w[0:1, :] * ext_ref[H - 2:H - 2 + T, :])

    for c in range(F // FC):
        half_gate = conv_cols(c * FC, 0.5)
        val = conv_cols(F + c * FC, 1.0)
        act_ref[:, c * FC:(c + 1) * FC] = (half_gate * (jnp.tanh(half_gate) + 1.0) * val).astype(act_ref.dtype)

    y = x_ref[0] + jnp.dot(act_ref[...], wdn_ref[...], preferred_element_type=jnp.float32)
    if final_norm:
        y = _rms_norm(y, gf_ref[...])
    o_ref[0] = y


def _ffn(x, g, wup, cw, cb, wdn, gf, layer, final_norm):
    B, S, D = x.shape
    F = wdn.shape[1]
    T = min(FFN_ROW_TILE, S)
    assert S % T == 0
    row = lambda b, s: (b, s, 0)
    return pl.pallas_call(
        functools.partial(_ffn_kernel, layer=layer, final_norm=final_norm),
        grid=(B, S // T),
        in_specs=[
            pl.BlockSpec((1, T, D), row),
            _whole(g.shape),
            _layer_operand(wup.shape[1:], layer),
            _layer_operand(cw.shape[1:], layer),
            _whole(cb.shape),
            _layer_operand(wdn.shape[1:], layer),
            _whole(gf.shape),
        ],
        out_specs=pl.BlockSpec((1, T, D), row),
        out_shape=jax.ShapeDtypeStruct((B, S, D), x.dtype),
        scratch_shapes=[
            pltpu.VMEM((CONV_HALO + T, FF_CHUNK), jnp.float32),
            pltpu.VMEM((CONV_HALO, 2 * F), jnp.float32),
            pltpu.VMEM((T, F), jnp.bfloat16),
        ],
        compiler_params=pltpu.CompilerParams(
            dimension_semantics=("parallel", "arbitrary"), vmem_limit_bytes=V7X_VMEM_LIMIT),
        name="ffn",
    )(x, g, wup, cw, cb, wdn, gf)


def kernel(x, norm_mix_g, w_in, w_pool, pool_scale, w_branch_a, w_branch_b, w_out, norm_ffn_g, w_up, conv_w, conv_b, w_down, norm_final_g):
    depth = w_in.shape[0]
    D = x.shape[-1]
    A = w_branch_a.shape[1]
    P = w_branch_b.shape[1]
    assert x.shape[1] % MOBA_BLOCK == 0 and A % (2 * HEAD_DIM) == 0
    assert w_down.shape[1] % FF_CHUNK == 0 and 2 * POOL_WINDOWS[-1] <= POOL_HALO and POOL_WINDOWS == (2, 4, 8, 16) and CONV_WIDTH - 1 <= CONV_HALO
    assert 3 * A + P == 2 * D, "the branch gates are column block 1 (width 2D) of w_in"
    whole = lambda w: (w.reshape(-1, w.shape[-1]), w.shape[-1], 0)
    early = (whole(w_branch_a), whole(w_pool), whole(w_branch_b), whole(w_out),
             (w_in.reshape(-1, w_in.shape[-1]), 2 * D, 1))
    late = (whole(w_up), whole(w_down))
    gf = norm_final_g.reshape(1, D)
    for layer in range(depth):
        q, k, vt, u, *cast = _in_proj(x, norm_mix_g, w_in, layer, A, P, casts=early if layer == 0 else ())
        if layer == 0:
            wa, wp, wb, wo = (c.reshape(w.shape) for c, w in zip(cast, (w_branch_a, w_pool, w_branch_b, w_out)))
            wg = cast[4].reshape(depth, D, 2 * D)
        a_lo, a_hi = _moba(q, k, vt)
        x, *cast = _merge(x, a_lo, a_hi, u, norm_mix_g, wg, wa, wp, pool_scale, wb, wo, layer,
                          casts=late if layer == 0 else ())
        if layer == 0:
            wup, wdn = (c.reshape(w.shape) for c, w in zip(cast, (w_up, w_down)))
        x = _ffn(x, norm_ffn_g, wup, conv_w, conv_b, wdn, gf, layer, layer == depth - 1)
    return x
```

```python
import functools

import jax
import jax.numpy as jnp
from jax import lax
from jax.experimental import pallas as pl
from jax.experimental.pallas import tpu as pltpu

HEAD_DIM = 64
MOBA_BLOCK = 256
MOBA_TOPK = 3
POOL_WINDOWS = (2, 4, 8, 16)
CONV_WIDTH = 3
RMS_EPS = 1e-6
NEG_INF = -1e30
LOG2_E = 1.4426950408889634

ROW_TILE = 1024
FFN_ROW_TILE = 512
SUB_TILE = 256
POOL_HALO = 32
ONES_ROWS = 16
CONV_HALO = 8
FF_CHUNK = 256
BF16_TILE_ROWS = 16
V7X_VMEM_LIMIT = 56 * 1024 * 1024

_NT = (((1,), (1,)), ((), ()))


def _layer_operand(block, layer, index=None):
    index = (0,) * len(block) if index is None else index
    return pl.BlockSpec((None,) + tuple(block), lambda b, s: (layer,) + tuple(index),
                        pipeline_mode=pl.Buffered(1))


def _whole(shape):
    return pl.BlockSpec(shape, lambda b, s: (0,) * len(shape), pipeline_mode=pl.Buffered(1))


def _cast_specs(arrays, grid):
    n_steps = grid[0] * grid[1]
    in_specs, out_specs, shapes = [], [], []
    for arr, cols, col_block in arrays:
        rows = arr.shape[0]
        assert rows % (BF16_TILE_ROWS * n_steps) == 0
        in_specs.append(pl.BlockSpec((rows // n_steps, cols), lambda b, s, j=col_block: (b * grid[1] + s, j)))
        out_specs.append(pl.BlockSpec((rows // n_steps, cols), lambda b, s: (b * grid[1] + s, 0)))
        shapes.append(jax.ShapeDtypeStruct((rows, cols), jnp.bfloat16))
    return in_specs, out_specs, shapes


def _cast_blocks(in_refs, out_refs):
    for src, dst in zip(in_refs, out_refs, strict=True):
        dst[...] = src[...].astype(dst.dtype)


def _rms_norm(x, g):
    y = x * lax.rsqrt(jnp.mean(x * x, axis=-1, keepdims=True) + RMS_EPS)
    return y * g


def _in_proj_kernel(x_ref, g_ref, wq32_ref, wk32_ref, wv32_ref, wu32_ref, *refs, scale, layer, n_cast):
    cast_in, (q_ref, k_ref, vt_ref, u_ref) = refs[:n_cast], refs[n_cast:n_cast + 4]
    cast_out, (wq_ref, wk_ref, wvt_ref, wu_ref) = refs[n_cast + 4:2 * n_cast + 4], refs[2 * n_cast + 4:]
    _cast_blocks(cast_in, cast_out)

    @pl.when((pl.program_id(0) == 0) & (pl.program_id(1) == 0))
    def _():
        _cast_blocks((wq32_ref, wk32_ref, wu32_ref), (wq_ref, wk_ref, wu_ref))
        wvt_ref[...] = wv32_ref[...].T.astype(wvt_ref.dtype)

    for j in range(vt_ref.shape[1]):
        rows = slice(j * MOBA_BLOCK, (j + 1) * MOBA_BLOCK)
        h = _rms_norm(x_ref[0, rows, :], g_ref[layer:layer + 1, :]).astype(jnp.bfloat16)
        q_ref[0, rows, :] = (jnp.dot(h, wq_ref[...], preferred_element_type=jnp.float32) * scale).astype(q_ref.dtype)
        k_ref[0, rows, :] = jnp.dot(h, wk_ref[...], preferred_element_type=jnp.float32).astype(k_ref.dtype)
        u_ref[0, rows, :] = jnp.dot(h, wu_ref[...], preferred_element_type=jnp.float32).astype(u_ref.dtype)
        vt_ref[0, j] = lax.dot_general(wvt_ref[...], h, _NT,
                                       preferred_element_type=jnp.float32).astype(vt_ref.dtype)


def _in_proj(x, g, w_in, layer, A, P, casts=()):
    B, S, D = x.shape
    assert P == A, "u is addressed as column block 3 of width A"
    T = min(ROW_TILE, S)
    assert S % T == 0 and T % SUB_TILE == 0
    nb_t = T // MOBA_BLOCK
    nb = S // MOBA_BLOCK
    grid = (B, S // T)
    cast_in_specs, cast_out_specs, cast_shapes = _cast_specs(casts, grid)
    return pl.pallas_call(
        functools.partial(_in_proj_kernel, scale=HEAD_DIM ** -0.5 * LOG2_E, layer=layer, n_cast=len(casts)),
        grid=grid,
        in_specs=[
            pl.BlockSpec((1, T, D), lambda b, s: (b, s, 0)),
            _whole(g.shape),
            _layer_operand((D, A), layer, (0, 0)),
            _layer_operand((D, A), layer, (0, 1)),
            _layer_operand((D, A), layer, (0, 2)),
            _layer_operand((D, P), layer, (0, 3)),
            *cast_in_specs,
        ],
        out_specs=[
            pl.BlockSpec((1, T, A), lambda b, s: (b, s, 0)),
            pl.BlockSpec((1, T, A), lambda b, s: (b, s, 0)),
            pl.BlockSpec((1, nb_t, A, MOBA_BLOCK), lambda b, s: (b, s, 0, 0)),
            pl.BlockSpec((1, T, P), lambda b, s: (b, s, 0)),
            *cast_out_specs,
        ],
        out_shape=[
            jax.ShapeDtypeStruct((B, S, A), jnp.bfloat16),
            jax.ShapeDtypeStruct((B, S, A), jnp.bfloat16),
            jax.ShapeDtypeStruct((B, nb, A, MOBA_BLOCK), jnp.bfloat16),
            jax.ShapeDtypeStruct((B, S, P), jnp.bfloat16),
            *cast_shapes,
        ],
        scratch_shapes=[pltpu.VMEM((D, A), jnp.bfloat16), pltpu.VMEM((D, A), jnp.bfloat16),
                        pltpu.VMEM((A, D), jnp.bfloat16), pltpu.VMEM((D, P), jnp.bfloat16)],
        compiler_params=pltpu.CompilerParams(
            dimension_semantics=("arbitrary", "arbitrary"), vmem_limit_bytes=V7X_VMEM_LIMIT),
        name="in_proj",
    )(x, g, w_in, w_in, w_in, w_in, *(c[0] for c in casts))


def _moba_kernel(qa_ref, qb_ref, k_ref, vt_ref, oa_ref, ob_ref, kmean_ref, qza_ref, qzb_ref,
                 s0_ref, mb0_ref, s1_ref, mb1_ref, *, n_heads, n_blocks):
    L = MOBA_BLOCK
    G = 2 * HEAD_DIM
    pair = pl.program_id(1)

    @pl.when(pair == 0)
    def _():
        row = lax.broadcasted_iota(jnp.int32, (n_blocks, n_blocks * L), 0)
        col = lax.broadcasted_iota(jnp.int32, (n_blocks, n_blocks * L), 1)
        ind = jnp.where((col >= row * L) & (col < (row + 1) * L), 1.0 / L, 0.0).astype(jnp.bfloat16)
        k_all = k_ref[0].reshape(n_blocks * L, k_ref.shape[-1])
        kmean_ref[...] = jnp.dot(ind, k_all, preferred_element_type=jnp.float32)

    blk = lax.broadcasted_iota(jnp.int32, (n_blocks, L), 0)
    key_pos = lax.broadcasted_iota(jnp.int32, (L, L), 0)
    qry_pos = lax.broadcasted_iota(jnp.int32, (L, L), 1)
    lane = lax.broadcasted_iota(jnp.int32, (L, G), 1)
    Hh = L // 2
    ones_rows = jnp.ones((ONES_ROWS, L), jnp.bfloat16)
    groups = [slice((h // 2) * G, (h // 2 + 1) * G) for h in range(n_heads)]
    slots = ((s0_ref, mb0_ref), (s1_ref, mb1_ref))

    def pv(b, h, p):
        lhs = jnp.concatenate([vt_ref[0, b, h * HEAD_DIM:(h + 1) * HEAD_DIM, :], ones_rows], axis=0)
        return jnp.dot(lhs, p.astype(jnp.bfloat16), preferred_element_type=jnp.float32)

    def stage_a(qz_ref, b, h, slot, causal):
        s_ref, mb_ref = slot
        if causal:
            tri = (lax.broadcasted_iota(jnp.int32, (Hh, Hh), 0) <= lax.broadcasted_iota(jnp.int32, (Hh, Hh), 1))
            s_l = lax.dot_general(k_ref[0, b, :Hh, groups[h]], qz_ref[h, :Hh], _NT,
                                  preferred_element_type=jnp.float32)
            s_l = jnp.where(tri, s_l, NEG_INF)
            s_r = lax.dot_general(k_ref[0, b, :, groups[h]], qz_ref[h, Hh:], _NT,
                                  preferred_element_type=jnp.float32)
            s_rl = jnp.where(tri, s_r[Hh:], NEG_INF)
            s_ref[h, :Hh, :Hh] = s_l
            s_ref[h, :Hh, Hh:] = s_r[:Hh]
            s_ref[h, Hh:, Hh:] = s_rl
            mb_ref[h:h + 1, :Hh] = jnp.max(s_l, axis=0, keepdims=True)
            mb_ref[h:h + 1, Hh:] = jnp.max(jnp.maximum(s_r[:Hh], s_rl), axis=0, keepdims=True)
            return
        s = lax.dot_general(k_ref[0, b, :, groups[h]], qz_ref[h], _NT, preferred_element_type=jnp.float32)
        s_ref[h] = s
        mb_ref[h:h + 1, :] = jnp.max(s, axis=0, keepdims=True)

    def stage_b(b, h, slot, state, on):
        s_ref, mb_ref = slot
        mb = mb_ref[h:h + 1, :]
        if state is None:
            lhs = jnp.concatenate([vt_ref[0, b, h * HEAD_DIM:(h + 1) * HEAD_DIM, :], ones_rows], axis=0)
            lhs_l = jnp.concatenate([vt_ref[0, b, h * HEAD_DIM:(h + 1) * HEAD_DIM, :Hh],
                                     jnp.ones((ONES_ROWS, Hh), jnp.bfloat16)], axis=0)
            p_l = jnp.exp2(s_ref[h, :Hh, :Hh] - mb_ref[h:h + 1, :Hh]).astype(jnp.bfloat16)
            p_r = jnp.exp2(s_ref[h, :, Hh:] - mb_ref[h:h + 1, Hh:]).astype(jnp.bfloat16)
            return mb, jnp.concatenate([jnp.dot(lhs_l, p_l, preferred_element_type=jnp.float32),
                                        jnp.dot(lhs, p_r, preferred_element_type=jnp.float32)], axis=1)
        m, acc = state
        if on is None:
            m_new = jnp.maximum(m, mb)
            shift = m_new
        else:
            m_new = jnp.where(on, jnp.maximum(m, mb), m)
            shift = jnp.where(on, m_new, -NEG_INF)
        return m_new, jnp.exp2(m - m_new) * acc + pv(b, h, jnp.exp2(s_ref[h] - shift))

    def prepare(ti, q_ref, qz_ref):
        sel = []
        for h in range(n_heads):
            in_head = (lane >= (h % 2) * HEAD_DIM) & (lane < (h % 2 + 1) * HEAD_DIM)
            qz_ref[h] = jnp.where(in_head, q_ref[0, :, groups[h]], jnp.zeros((L, G), q_ref.dtype))
            if ti <= MOBA_TOPK:
                sel.append(None)
                continue
            km = kmean_ref[:, groups[h]]
            km_hi = km.astype(jnp.bfloat16)
            km_lo = (km - km_hi.astype(jnp.float32)).astype(jnp.bfloat16)
            g2 = lax.dot_general(jnp.concatenate([km_hi, km_lo], axis=0), qz_ref[h], _NT,
                                 preferred_element_type=jnp.float32)
            gate = g2[:n_blocks] + g2[n_blocks:]
            rank = jnp.zeros((n_blocks, L), jnp.float32)
            for j in range(ti):
                gj = gate[j:j + 1, :]
                rank = rank + jnp.where((gj > gate) | ((gj == gate) & (blk > j)), 1.0, 0.0)
            sel.append(jnp.where(rank < MOBA_TOPK, 1.0, 0.0))
        return sel

    def finalize(state, o_ref):
        for h in range(n_heads):
            acc = state[h][1]
            inv_l = 1.0 / acc[HEAD_DIM:HEAD_DIM + 1, :]
            o_ref[0, 0, h * HEAD_DIM:(h + 1) * HEAD_DIM, :] = (acc[0:HEAD_DIM, :] * inv_l).astype(o_ref.dtype)

    def tile_pair(tp):
        tiles = ((tp, qa_ref, qza_ref, oa_ref), (n_blocks - 1 - tp, qb_ref, qzb_ref, ob_ref))
        sels = [prepare(ti, q_ref, qz_ref) for ti, q_ref, qz_ref, _ in tiles]
        visits = [(w, b, n == 0, n == tiles[w][0]) for w in range(2)
                  for n, b in enumerate([tiles[w][0]] + list(range(tiles[w][0])))]
        states = [[None] * n_heads, [None] * n_heads]
        for h in range(n_heads):
            stage_a(tiles[0][2], visits[0][1], h, slots[0], causal=True)
        for n, (w, b, first, last) in enumerate(visits):
            for h in range(n_heads):
                if n + 1 < len(visits):
                    w2, b2, first2, _ = visits[n + 1]
                    stage_a(tiles[w2][2], b2, h, slots[(n + 1) % 2], causal=first2)
                on = None if (first or sels[w][h] is None) else sels[w][h][b:b + 1, :] > 0.0
                states[w][h] = stage_b(b, h, slots[n % 2], states[w][h], on)
            if last:
                finalize(states[w], tiles[w][3])

    for tp in range(n_blocks // 2):
        pl.when(pair == tp)(functools.partial(tile_pair, tp))


def _moba(q, k, vt):
    B, S, A = q.shape
    L = MOBA_BLOCK
    nb = S // L
    H = A // HEAD_DIM
    assert nb % 2 == 0
    k4 = k.reshape(B, nb, L, A)
    half = jax.ShapeDtypeStruct((B, nb // 2, A, L), jnp.bfloat16)
    return pl.pallas_call(
        functools.partial(_moba_kernel, n_heads=H, n_blocks=nb),
        grid=(B, nb // 2),
        in_specs=[
            pl.BlockSpec((1, L, A), lambda b, j: (b, j, 0)),
            pl.BlockSpec((1, L, A), lambda b, j: (b, nb - 1 - j, 0)),
            pl.BlockSpec((1, nb, L, A), lambda b, j: (b, 0, 0, 0)),
            pl.BlockSpec((1, nb, A, L), lambda b, j: (b, 0, 0, 0)),
        ],
        out_specs=[
            pl.BlockSpec((1, 1, A, L), lambda b, j: (b, j, 0, 0)),
            pl.BlockSpec((1, 1, A, L), lambda b, j: (b, nb // 2 - 1 - j, 0, 0)),
        ],
        out_shape=[half, half],
        scratch_shapes=[
            pltpu.VMEM((nb, A), jnp.float32),
            pltpu.VMEM((H, L, 2 * HEAD_DIM), jnp.bfloat16),
            pltpu.VMEM((H, L, 2 * HEAD_DIM), jnp.bfloat16),
            pltpu.VMEM((H, L, L), jnp.float32),
            pltpu.VMEM((H, L), jnp.float32),
            pltpu.VMEM((H, L, L), jnp.float32),
            pltpu.VMEM((H, L), jnp.float32),
        ],
        compiler_params=pltpu.CompilerParams(
            dimension_semantics=("parallel", "arbitrary"), vmem_limit_bytes=V7X_VMEM_LIMIT),
        name="moba",
    )(q, q, k4, vt)


def _merge_kernel(x_ref, a_lo_ref, a_hi_ref, u_ref, g_ref, wg_ref, wa_ref, wp_ref, ps_ref, wb_ref, wo_ref,
                  *refs, layer, n_lo, n_cast):
    cast_in, o_ref, cast_out = refs[:n_cast], refs[n_cast], refs[n_cast + 1:2 * n_cast + 1]
    uext_ref, w2_ref, w4_ref, w8_ref = refs[2 * n_cast + 1:]
    _cast_blocks(cast_in, cast_out)
    T = x_ref.shape[1]
    D = x_ref.shape[2]
    GW = wp_ref.shape[1]
    R = SUB_TILE
    HALO = POOL_HALO
    s_idx = pl.program_id(1)

    @pl.when(s_idx == 0)
    def _():
        uext_ref[0:HALO, :] = jnp.zeros((HALO, uext_ref.shape[1]), jnp.float32)

    def branch_in(r0):
        at = jnp.where(s_idx < n_lo, a_lo_ref[0, r0 // R], a_hi_ref[0, r0 // R])
        y_a = lax.dot_general(at, wa_ref[...], (((0,), (0,)), ((), ())), preferred_element_type=jnp.float32)
        h_half = _rms_norm(x_ref[0, r0:r0 + R, :], 0.5 * g_ref[layer:layer + 1, :]).astype(jnp.bfloat16)
        return y_a, jnp.dot(h_half, wg_ref[...], preferred_element_type=jnp.float32)

    def window_sums():
        n = HALO + T
        uext_ref[HALO:n, :] = u_ref[0].astype(jnp.float32)
        w2_ref[8:n, :] = uext_ref[8:n, :] + uext_ref[7:n - 1, :]
        w4_ref[16:n, GW:] = w2_ref[16:n, GW:] + w2_ref[14:n - 2, GW:]
        w8_ref[24:n, 2 * GW:] = w4_ref[24:n, 2 * GW:] + w4_ref[20:n - 4, 2 * GW:]

    def branch_out(r0, y_a, g_half):
        e0 = HALO + r0
        wsums = (w2_ref[e0:e0 + R, 0:GW], w4_ref[e0:e0 + R, GW:2 * GW], w8_ref[e0:e0 + R, 2 * GW:3 * GW],
                 w8_ref[e0:e0 + R, 3 * GW:] + w8_ref[e0 - 8:e0 - 8 + R, 3 * GW:])
        t_pos = s_idx * T + r0 + lax.broadcasted_iota(jnp.int32, (R, GW), 0)
        ys = []
        for gi, win in enumerate(POOL_WINDOWS):
            if r0 >= win:
                mean = wsums[gi] * (1.0 / win)
            else:
                mean = wsums[gi] / jnp.minimum(t_pos + 1, win).astype(jnp.float32)
            mixed = (mean - uext_ref[e0:e0 + R, gi * GW:(gi + 1) * GW]).astype(jnp.bfloat16)
            ys.append(jnp.dot(mixed, wp_ref[gi], preferred_element_type=jnp.float32))
        pooled = (jnp.concatenate(ys, axis=1) * ps_ref[layer:layer + 1, :]).astype(jnp.bfloat16)
        y_b = jnp.dot(pooled, wb_ref[...], preferred_element_type=jnp.float32)
        t = jnp.tanh(g_half)
        merged = (0.5 * ((t[:, :D] + 1.0) * y_a + (t[:, D:] + 1.0) * y_b)).astype(jnp.bfloat16)
        o_ref[0, r0:r0 + R, :] = x_ref[0, r0:r0 + R, :] + jnp.dot(merged, wo_ref[...],
                                                                    preferred_element_type=jnp.float32)

    n_sub = T // R
    pending = branch_in(0)
    window_sums()
    for j in range(n_sub):
        nxt = branch_in((j + 1) * R) if j + 1 < n_sub else None
        branch_out(j * R, *pending)
        pending = nxt
    uext_ref[0:HALO, :] = uext_ref[T:T + HALO, :]


def _merge(x, a_lo, a_hi, u, g, wg, wa, wp, ps, wb, wo, layer, casts=()):
    B, S, D = x.shape
    A = a_lo.shape[2]
    assert a_lo.shape[3] == SUB_TILE
    P = u.shape[2]
    T = min(ROW_TILE, S)
    assert S % T == 0 and T % SUB_TILE == 0
    row = lambda b, s: (b, s, 0)
    n_lo = S // T // 2
    assert S % (2 * T) == 0
    grid = (B, S // T)
    cast_in_specs, cast_out_specs, cast_shapes = _cast_specs(casts, grid)
    return pl.pallas_call(
        functools.partial(_merge_kernel, layer=layer, n_lo=n_lo, n_cast=len(casts)),
        grid=grid,
        in_specs=[
            pl.BlockSpec((1, T, D), row),
            pl.BlockSpec((1, T // SUB_TILE, A, SUB_TILE), lambda b, s: (b, jnp.minimum(s, n_lo - 1), 0, 0)),
            pl.BlockSpec((1, T // SUB_TILE, A, SUB_TILE), lambda b, s: (b, jnp.maximum(s - n_lo, 0), 0, 0)),
            pl.BlockSpec((1, T, P), row),
            _whole(g.shape),
            _layer_operand((D, 2 * D), layer),
            _layer_operand(wa.shape[1:], layer),
            _layer_operand(wp.shape[1:], layer),
            _whole(ps.shape),
            _layer_operand(wb.shape[1:], layer),
            _layer_operand(wo.shape[1:], layer),
            *cast_in_specs,
        ],
        out_specs=[pl.BlockSpec((1, T, D), row), *cast_out_specs],
        out_shape=[jax.ShapeDtypeStruct((B, S, D), x.dtype), *cast_shapes],
        scratch_shapes=[pltpu.VMEM((POOL_HALO + T, P), jnp.float32)] * 4,
        compiler_params=pltpu.CompilerParams(
            dimension_semantics=("parallel", "arbitrary"), vmem_limit_bytes=V7X_VMEM_LIMIT),
        name="merge",
    )(x, a_lo, a_hi, u, g, wg, wa, wp, ps, wb, wo, *(c[0] for c in casts))


def _ffn_kernel(x_ref, g_ref, wup_ref, cw_ref, cb_ref, wdn_ref, gf_ref, o_ref,
                ext_ref, carry_ref, act_ref, *, layer, final_norm):
    T = x_ref.shape[1]
    F = wdn_ref.shape[0]
    FC = FF_CHUNK
    H = CONV_HALO
    s_idx = pl.program_id(1)

    @pl.when(s_idx == 0)
    def _():
        carry_ref[...] = jnp.zeros(carry_ref.shape, jnp.float32)

    h = _rms_norm(x_ref[0], g_ref[layer:layer + 1, :]).astype(jnp.bfloat16)

    def conv_cols(c0, scale):
        ext_ref[0:H, :] = carry_ref[:, c0:c0 + FC]
        ext_ref[H:H + T, :] = jnp.dot(h, wup_ref[:, c0:c0 + FC], preferred_element_type=jnp.float32)
        carry_ref[:, c0:c0 + FC] = ext_ref[T:T + H, :]
        w = cw_ref[:, c0:c0 + FC] * scale
        return (cb_ref[layer:layer + 1, c0:c0 + FC] * scale + w[2:3, :] * ext_ref[H:H + T, :]
                + w[1:2, :] * ext_ref[H - 1:H - 1 + T, :]
                + w[0:1, :] * ext_ref[H - 2:H - 2 + T, :])

    for c in range(F // FC):
        half_gate = conv_cols(c * FC, 0.5)
        val = conv_cols(F + c * FC, 1.0)
        act_ref[:, c * FC:(c + 1) * FC] = (half_gate * (jnp.tanh(half_gate) + 1.0) * val).astype(act_ref.dtype)

    y = x_ref[0] + jnp.dot(act_ref[...], wdn_ref[...], preferred_element_type=jnp.float32)
    if final_norm:
        y = _rms_norm(y, gf_ref[...])
    o_ref[0] = y


def _ffn(x, g, wup, cw, cb, wdn, gf, layer, final_norm):
    B, S, D = x.shape
    F = wdn.shape[1]
    T = min(FFN_ROW_TILE, S)
    assert S % T == 0
    row = lambda b, s: (b, s, 0)
    return pl.pallas_call(
        functools.partial(_ffn_kernel, layer=layer, final_norm=final_norm),
        grid=(B, S // T),
        in_specs=[
            pl.BlockSpec((1, T, D), row),
            _whole(g.shape),
            _layer_operand(wup.shape[1:], layer),
            _layer_operand(cw.shape[1:], layer),
            _whole(cb.shape),
            _layer_operand(wdn.shape[1:], layer),
            _whole(gf.shape),
        ],
        out_specs=pl.BlockSpec((1, T, D), row),
        out_shape=jax.ShapeDtypeStruct((B, S, D), x.dtype),
        scratch_shapes=[
            pltpu.VMEM((CONV_HALO + T, FF_CHUNK), jnp.float32),
            pltpu.VMEM((CONV_HALO, 2 * F), jnp.float32),
            pltpu.VMEM((T, F), jnp.bfloat16),
        ],
        compiler_params=pltpu.CompilerParams(
            dimension_semantics=("parallel", "arbitrary"), vmem_limit_bytes=V7X_VMEM_LIMIT),
        name="ffn",
    )(x, g, wup, cw, cb, wdn, gf)


def kernel(x, norm_mix_g, w_in, w_pool, pool_scale, w_branch_a, w_branch_b, w_out, norm_ffn_g, w_up, conv_w, conv_b, w_down, norm_final_g):
    depth = w_in.shape[0]
    D = x.shape[-1]
    A = w_branch_a.shape[1]
    P = w_branch_b.shape[1]
    assert x.shape[1] % MOBA_BLOCK == 0 and A % (2 * HEAD_DIM) == 0
    assert w_down.shape[1] % FF_CHUNK == 0 and 2 * POOL_WINDOWS[-1] <= POOL_HALO and POOL_WINDOWS == (2, 4, 8, 16) and CONV_WIDTH - 1 <= CONV_HALO
    assert 3 * A + P == 2 * D, "the branch gates are column block 1 (width 2D) of w_in"
    whole = lambda w: (w.reshape(-1, w.shape[-1]), w.shape[-1], 0)
    early = (whole(w_branch_a), whole(w_pool), whole(w_branch_b), whole(w_out),
             (w_in.reshape(-1, w_in.shape[-1]), 2 * D, 1))
    late = (whole(w_up), whole(w_down))
    gf = norm_final_g.reshape(1, D)
    for layer in range(depth):
        q, k, vt, u, *cast = _in_proj(x, norm_mix_g, w_in, layer, A, P, casts=early if layer == 0 else ())
        if layer == 0:
            wa, wp, wb, wo = (c.reshape(w.shape) for c, w in zip(cast, (w_branch_a, w_pool, w_branch_b, w_out)))
            wg = cast[4].reshape(depth, D, 2 * D)
        a_lo, a_hi = _moba(q, k, vt)
        x, *cast = _merge(x, a_lo, a_hi, u, norm_mix_g, wg, wa, wp, pool_scale, wb, wo, layer,
                          casts=late if layer == 0 else ())
        if layer == 0:
            wup, wdn = (c.reshape(w.shape) for c, w in zip(cast, (w_up, w_down)))
        x = _ffn(x, norm_ffn_g, wup, conv_w, conv_b, wdn, gf, layer, layer == depth - 1)
    return x
```
